```python
import jax, jax.numpy as jnp
from jax import lax
import numpy as np

D_MODEL = 2048
BATCH = 2
SEQ = 4096
DEPTH = 1
DEC_BATCH = 128
DEC_SEQ = 4
PAST_LEN = 16384
PAGE_SIZE = 128

POOL_GROUPS = 4
POOL_WINDOWS = (2, 4, 8, 16)
POOL_GROUP_DIM = 256
POOL_DIM = POOL_GROUPS * POOL_GROUP_DIM
POOL_OUT_GROUP_DIM = D_MODEL // POOL_GROUPS
POOL_STATE_LEN = max(POOL_WINDOWS) - 1
V_DIM = 128
N_HEADS = D_MODEL // V_DIM
Q_LORA = 512
KV_LORA = 512
NOPE_DIM = 128
ROPE_DIM = 64
ROPE_THETA = 10000.0
SOFTMAX_SCALE = (NOPE_DIM + ROPE_DIM) ** -0.5
Q_BLOCK = 128
IN_DIM = POOL_DIM + Q_LORA + KV_LORA + ROPE_DIM + 2 * D_MODEL
N_GROUPS = 8
EXPERTS_PER_GROUP = 8
N_EXPERTS = N_GROUPS * EXPERTS_PER_GROUP
TOP_K = 2
D_EXPERT = 1408
MOE_BLOCK = 128
EPS = 1e-6

kernel_name = 'pool_mla_hier_moe_decoder_step'


def _rmsnorm(x, g):
    xf = x.astype(jnp.float32)
    xf = xf * lax.rsqrt(jnp.mean(xf * xf, axis=-1, keepdims=True) + EPS)
    return (xf * g.astype(jnp.float32)).astype(x.dtype)


def _rope_tables(pos):
    inv = 1.0 / (ROPE_THETA ** (jnp.arange(0, ROPE_DIM, 2, dtype=jnp.float32) / ROPE_DIM))
    ang = pos.astype(jnp.float32)[:, None] * inv[None, :]
    return jnp.cos(ang), jnp.sin(ang)


def _apply_rope(x, cos, sin):
    xf = x.astype(jnp.float32)
    x1, x2 = xf[..., :ROPE_DIM // 2], xf[..., ROPE_DIM // 2:]
    return jnp.concatenate([x1 * cos - x2 * sin, x2 * cos + x1 * sin], axis=-1).astype(x.dtype)


def _mixer_inputs(xn, w_in, g_q_norm, g_kv_norm, w_uq, cos, sin):
    h = jnp.einsum('bsd,de->bse', xn, w_in)
    o1 = POOL_DIM
    o2 = o1 + Q_LORA
    o3 = o2 + KV_LORA
    o4 = o3 + ROPE_DIM
    o5 = o4 + D_MODEL
    u = h[..., :o1]
    q_lat = _rmsnorm(h[..., o1:o2], g_q_norm)
    kv = _rmsnorm(h[..., o2:o3], g_kv_norm)
    kr = _apply_rope(h[..., o3:o4], cos, sin)
    gate_pool = jax.nn.sigmoid(h[..., o4:o5])
    gate_mla = jax.nn.sigmoid(h[..., o5:])
    q = jnp.einsum('bsc,chd->bshd', q_lat, w_uq)
    q_nope = q[..., :NOPE_DIM]
    q_rope = _apply_rope(q[..., NOPE_DIM:], cos[:, None], sin[:, None])
    return u, q_nope, q_rope, kv, kr, gate_pool, gate_mla


def _pool_mix(u_full, pos, w_pool, pool_scale):
    B, T, _ = u_full.shape
    S = T - POOL_STATE_LEN
    cs = jnp.cumsum(u_full.astype(jnp.float32), axis=1)
    cs = jnp.pad(cs, ((0, 0), (1, 0), (0, 0)))
    u_new = u_full[:, POOL_STATE_LEN:].astype(jnp.float32)
    outs = []
    for gi, w in enumerate(POOL_WINDOWS):
        c0, c1 = gi * POOL_GROUP_DIM, (gi + 1) * POOL_GROUP_DIM
        hi = cs[:, POOL_STATE_LEN + 1:, c0:c1]
        lo = cs[:, POOL_STATE_LEN + 1 - w:T + 1 - w, c0:c1]
        count = jnp.minimum(w, pos + 1).astype(jnp.float32)[None, :, None]
        outs.append((hi - lo) / count - u_new[..., c0:c1])
    pooled = jnp.stack(outs, axis=2).astype(u_full.dtype)
    y = jnp.einsum('bsgc,gcd->bsgd', pooled, w_pool).reshape(B, S, D_MODEL)
    return y * pool_scale


def _attend_prompt(q_nope, q_rope, kv, kr, w_uk, w_uv):
    B, S, H, _ = q_nope.shape
    nb = S // Q_BLOCK
    k_nope = jnp.einsum('btc,chn->bthn', kv, w_uk)
    v = jnp.einsum('btc,chv->bthv', kv, w_uv)
    key_pos = jnp.arange(S)
    qn_b = q_nope.reshape(B, nb, Q_BLOCK, H, NOPE_DIM).swapaxes(0, 1)
    qr_b = q_rope.reshape(B, nb, Q_BLOCK, H, ROPE_DIM).swapaxes(0, 1)

    def one_block(args):
        qn, qr, i = args
        q_pos = i * Q_BLOCK + jnp.arange(Q_BLOCK)
        mask = key_pos[None, :] <= q_pos[:, None]
        s = (jnp.einsum('bqhn,bthn->bhqt', qn, k_nope, preferred_element_type=jnp.float32)
             + jnp.einsum('bqhr,btr->bhqt', qr, kr, preferred_element_type=jnp.float32)) * SOFTMAX_SCALE
        s = jnp.where(mask[None, None], s, -jnp.inf)
        p = jax.nn.softmax(s, axis=-1).astype(v.dtype)
        return jnp.einsum('bhqt,bthv->bqhv', p, v)

    o = lax.map(one_block, (qn_b, qr_b, jnp.arange(nb)))
    return o.swapaxes(0, 1).reshape(B, S, H, V_DIM)


def _attend_sample(q_nope, q_rope, kv_new, kr_new, w_uk, w_uv, cache_kv_latent, cache_k_rope, page_table, l):
    Sd = q_nope.shape[1]
    past_len = page_table.shape[1] * PAGE_SIZE
    q_abs = jnp.einsum('bshn,chn->bshc', q_nope, w_uk)
    key_pos = jnp.arange(past_len + Sd)
    q_pos = past_len + jnp.arange(Sd)
    mask = key_pos[None, :] <= q_pos[:, None]

    def one_sequence(args):
        qa, qr, kvn, krn, pages = args
        kv = jnp.concatenate([cache_kv_latent[l, pages].reshape(past_len, KV_LORA), kvn], axis=0)
        kr = jnp.concatenate([cache_k_rope[l, pages].reshape(past_len, ROPE_DIM), krn], axis=0)
        s = (jnp.einsum('qhc,tc->hqt', qa, kv, preferred_element_type=jnp.float32)
             + jnp.einsum('qhr,tr->hqt', qr, kr, preferred_element_type=jnp.float32)) * SOFTMAX_SCALE
        s = jnp.where(mask[None], s, -jnp.inf)
        p = jax.nn.softmax(s, axis=-1).astype(kv.dtype)
        return jnp.einsum('hqt,tc->qhc', p, kv)

    o_lat = lax.map(one_sequence, (q_abs, q_rope, kv_new, kr_new, page_table))
    return jnp.einsum('bshc,chv->bshv', o_lat, w_uv)


def _route(xn, w_group_router, b_group_router, w_router, b_router):
    N = xn.shape[0]
    gl = jnp.einsum('nd,dg->ng', xn, w_group_router).astype(jnp.float32) + b_group_router.astype(jnp.float32)
    gp = jax.nn.softmax(gl, axis=-1)
    g_sel = jnp.argmax(gl, axis=-1)
    g_w = jnp.take_along_axis(gp, g_sel[:, None], axis=-1)
    el = (jnp.einsum('nd,de->ne', xn, w_router).astype(jnp.float32) + b_router.astype(jnp.float32))
    el = el.reshape(N, N_GROUPS, EXPERTS_PER_GROUP)
    el_sel = jnp.take_along_axis(el, g_sel[:, None, None], axis=1)[:, 0]
    top_l, top_i = lax.top_k(el_sel, TOP_K)
    weights = g_w * jax.nn.softmax(top_l, axis=-1)
    experts = g_sel[:, None].astype(jnp.int32) * EXPERTS_PER_GROUP + top_i.astype(jnp.int32)
    return experts, weights


def _moe(xn, l, w_group_router, b_group_router, w_router, b_router, w_gate_e, w_up_e, w_down_e):
    N, D = xn.shape
    experts, weights = _route(xn, w_group_router[l], b_group_router[l], w_router[l], b_router[l])
    A = N * TOP_K
    e_flat = experts.reshape(A)
    w_flat = weights.reshape(A)
    tok_flat = jnp.arange(A, dtype=jnp.int32) // TOP_K
    order = jnp.argsort(e_flat)
    e_s, tok_s, w_s = e_flat[order], tok_flat[order], w_flat[order]
    counts = jnp.zeros((N_EXPERTS,), jnp.int32).at[e_flat].add(1)
    padded = (counts + MOE_BLOCK - 1) // MOE_BLOCK * MOE_BLOCK
    start = jnp.cumsum(counts) - counts
    pend = jnp.cumsum(padded)
    pstart = pend - padded
    dest = pstart[e_s] + jnp.arange(A, dtype=jnp.int32) - start[e_s]
    nb = (A + N_EXPERTS * (MOE_BLOCK - 1) + MOE_BLOCK - 1) // MOE_BLOCK
    n_slots = nb * MOE_BLOCK
    slot_tok = jnp.full((n_slots,), N, jnp.int32).at[dest].set(tok_s)
    slot_w = jnp.zeros((n_slots,), jnp.float32).at[dest].set(w_s)
    block_expert = jnp.minimum(jnp.searchsorted(pend, jnp.arange(nb, dtype=jnp.int32) * MOE_BLOCK, side='right'), N_EXPERTS - 1)
    x_pad = jnp.concatenate([xn, jnp.zeros((1, D), xn.dtype)], axis=0)
    xb = x_pad[slot_tok].reshape(nb, MOE_BLOCK, D)

    def expert_block(args):
        xblk, e = args
        h = jax.nn.silu(xblk @ w_gate_e[l, e]) * (xblk @ w_up_e[l, e])
        return h @ w_down_e[l, e]

    yb = lax.map(expert_block, (xb, block_expert)).reshape(n_slots, D)
    out = jnp.zeros((N + 1, D), jnp.float32).at[slot_tok].add(yb.astype(jnp.float32) * slot_w[:, None])
    return out[:N].astype(xn.dtype)


def _trunk_layer(x, pos, cos, sin, u_prefix, attend, l, g_mix_norm, w_in, g_q_norm, g_kv_norm, w_uq,
                 w_pool, pool_scale, w_o, g_ffn_norm, w_group_router, b_group_router, w_router, b_router,
                 w_gate_e, w_up_e, w_down_e):
    B, S, D = x.shape
    xn = _rmsnorm(x, g_mix_norm[l])
    u, q_nope, q_rope, kv, kr, gate_pool, gate_mla = _mixer_inputs(
        xn, w_in[l], g_q_norm[l], g_kv_norm[l], w_uq[l], cos, sin)
    u_full = jnp.concatenate([u_prefix, u], axis=1)
    y_pool = _pool_mix(u_full, pos, w_pool[l], pool_scale[l])
    y_mla = attend(q_nope, q_rope, kv, kr).reshape(B, S, D)
    merged = gate_pool * y_pool + gate_mla * y_mla
    x = x + jnp.einsum('bsd,de->bse', merged, w_o[l])
    hn = _rmsnorm(x, g_ffn_norm[l]).reshape(B * S, D)
    x = x + _moe(hn, l, w_group_router, b_group_router, w_router, b_router,
                 w_gate_e, w_up_e, w_down_e).reshape(B, S, D)
    return x, kv, kr, u_full[:, -POOL_STATE_LEN:]


def setup_inputs(seed: int = 0) -> dict:
    key = jax.random.key(seed)
    ks = jax.random.split(key, 26)
    f32 = jnp.float32
    L = DEPTH
    n_pages = PAST_LEN // PAGE_SIZE
    n_used = DEC_BATCH * n_pages
    n_pool = (5 * n_used + 3) // 4

    def nrm(k, shape, scale):
        return jax.random.normal(k, shape, f32) * scale

    def gain(k, shape):
        return 1.0 + 0.01 * jax.random.normal(k, shape, f32)

    page_table = jax.random.permutation(ks[5], n_pool)[:n_used].reshape(DEC_BATCH, n_pages).astype(jnp.int32)
    return {
        'x_prompt': nrm(ks[0], (BATCH, SEQ, D_MODEL), 1.0),
        'x_sample': nrm(ks[1], (DEC_BATCH, DEC_SEQ, D_MODEL), 1.0),
        'state_pool': nrm(ks[2], (L, DEC_BATCH, POOL_STATE_LEN, POOL_DIM), 1.0),
        'cache_kv_latent': nrm(ks[3], (L, n_pool, PAGE_SIZE, KV_LORA), 1.0),
        'cache_k_rope': nrm(ks[4], (L, n_pool, PAGE_SIZE, ROPE_DIM), 1.0),
        'page_table': page_table,
        'g_mix_norm': gain(ks[6], (L, D_MODEL)),
        'w_in': nrm(ks[7], (L, D_MODEL, IN_DIM), D_MODEL ** -0.5),
        'g_q_norm': gain(ks[8], (L, Q_LORA)),
        'g_kv_norm': gain(ks[9], (L, KV_LORA)),
        'w_uq': nrm(ks[10], (L, Q_LORA, N_HEADS, NOPE_DIM + ROPE_DIM), Q_LORA ** -0.5),
        'w_uk': nrm(ks[11], (L, KV_LORA, N_HEADS, NOPE_DIM), KV_LORA ** -0.5),
        'w_uv': nrm(ks[12], (L, KV_LORA, N_HEADS, V_DIM), KV_LORA ** -0.5),
        'w_pool': nrm(ks[13], (L, POOL_GROUPS, POOL_GROUP_DIM, POOL_OUT_GROUP_DIM), POOL_GROUP_DIM ** -0.5),
        'pool_scale': 1.0 + 0.1 * jax.random.normal(ks[14], (L, D_MODEL), f32),
        'w_o': nrm(ks[15], (L, D_MODEL, D_MODEL), D_MODEL ** -0.5),
        'g_ffn_norm': gain(ks[16], (L, D_MODEL)),
        'w_group_router': nrm(ks[17], (L, D_MODEL, N_GROUPS), D_MODEL ** -0.5),
        'b_group_router': nrm(ks[18], (L, N_GROUPS), 0.01),
        'w_router': nrm(ks[19], (L, D_MODEL, N_EXPERTS), D_MODEL ** -0.5),
        'b_router': nrm(ks[20], (L, N_EXPERTS), 0.01),
        'w_gate_e': nrm(ks[21], (L, N_EXPERTS, D_MODEL, D_EXPERT), D_MODEL ** -0.5),
        'w_up_e': nrm(ks[22], (L, N_EXPERTS, D_MODEL, D_EXPERT), D_MODEL ** -0.5),
        'w_down_e': nrm(ks[23], (L, N_EXPERTS, D_EXPERT, D_MODEL), D_EXPERT ** -0.5),
        'g_final_norm': gain(ks[24], (D_MODEL,)),
    }


def reference(x_prompt, x_sample, state_pool, cache_kv_latent, cache_k_rope, page_table,
              g_mix_norm, w_in, g_q_norm, g_kv_norm, w_uq, w_uk, w_uv, w_pool, pool_scale, w_o,
              g_ffn_norm, w_group_router, b_group_router, w_router, b_router,
              w_gate_e, w_up_e, w_down_e, g_final_norm):
    B, S, _ = x_prompt.shape
    Sd = x_sample.shape[1]
    past_len = page_table.shape[1] * PAGE_SIZE
    pos_p = jnp.arange(S, dtype=jnp.int32)
    pos_s = past_len + jnp.arange(Sd, dtype=jnp.int32)
    cos_p, sin_p = _rope_tables(pos_p)
    cos_s, sin_s = _rope_tables(pos_s)
    xp, xs = x_prompt, x_sample
    kv_p_rows, kr_p_rows, pool_p_rows = [], [], []
    kv_s_rows, kr_s_rows, pool_s_rows = [], [], []
    for l in range(DEPTH):
        def attend_p(qn, qr, kv, kr, l=l):
            return _attend_prompt(qn, qr, kv, kr, w_uk[l], w_uv[l])

        def attend_s(qn, qr, kv, kr, l=l):
            return _attend_sample(qn, qr, kv, kr, w_uk[l], w_uv[l], cache_kv_latent, cache_k_rope, page_table, l)

        prefix_p = jnp.zeros((B, POOL_STATE_LEN, POOL_DIM), xp.dtype)
        xp, kv_p, kr_p, pool_p = _trunk_layer(
            xp, pos_p, cos_p, sin_p, prefix_p, attend_p, l, g_mix_norm, w_in, g_q_norm, g_kv_norm, w_uq,
            w_pool, pool_scale, w_o, g_ffn_norm, w_group_router, b_group_router, w_router, b_router,
            w_gate_e, w_up_e, w_down_e)
        xs, kv_s, kr_s, pool_s = _trunk_layer(
            xs, pos_s, cos_s, sin_s, state_pool[l].astype(xs.dtype), attend_s, l, g_mix_norm, w_in, g_q_norm,
            g_kv_norm, w_uq, w_pool, pool_scale, w_o, g_ffn_norm, w_group_router, b_group_router, w_router,
            b_router, w_gate_e, w_up_e, w_down_e)
        kv_p_rows.append(kv_p)
        kr_p_rows.append(kr_p)
        pool_p_rows.append(pool_p)
        kv_s_rows.append(kv_s)
        kr_s_rows.append(kr_s)
        pool_s_rows.append(pool_s)
    y_prompt = _rmsnorm(xp, g_final_norm)
    y_sample = _rmsnorm(xs, g_final_norm)
    kv_prompt = jnp.stack(kv_p_rows)
    kr_prompt = jnp.stack(kr_p_rows)
    pool_prompt = jnp.stack(pool_p_rows)
    kv_sample = jnp.stack(kv_s_rows)
    kr_sample = jnp.stack(kr_s_rows)
    pool_sample = jnp.stack(pool_s_rows)
    return (y_prompt, y_sample, kv_prompt, kr_prompt, pool_prompt, kv_sample, kr_sample, pool_sample)
```

```python
import functools

import jax
import jax.numpy as jnp
from jax import lax
from jax.experimental import pallas as pl
from jax.experimental.pallas import tpu as pltpu

F32 = jnp.float32
BF16 = jnp.bfloat16

D_MODEL = 2048
POOL_WINDOWS = (2, 4, 8, 16)
POOL_GROUP_DIM = 256
POOL_DIM = 1024
POOL_OUT_GROUP_DIM = 512
POOL_STATE_LEN = 15
N_HEADS = 16
V_DIM = 128
Q_LORA = 512
KV_LORA = 512
NOPE_DIM = 128
ROPE_DIM = 64
HALF_ROPE = ROPE_DIM // 2
ROPE_THETA = 10000.0
SOFTMAX_SCALE = (NOPE_DIM + ROPE_DIM) ** -0.5
PAGE_SIZE = 128
N_GROUPS = 8
EXPERTS_PER_GROUP = 8
N_EXPERTS = 64
TOP_K = 2
D_EXPERT = 1408
EPS = 1e-6

LANES = 128
QK_DIM = 2 * LANES
HALO = 16
ROW_TILE = 256
POST_TILE = 128
ATTN_TILE = 256
PAGES_PER_STEP = 8
NEW_ROWS_PAD = 16
MOE_BLOCK = 384
FF_CHUNK = 128
COMBINE_TILE = 128
VMEM_LIMIT = 56 * 1024 * 1024


def _rms(xf, g):
    ms = jnp.mean(xf * xf, axis=-1, keepdims=True)
    return xf * lax.rsqrt(ms + EPS) * g


def _dot(a, b):
    return jnp.dot(a, b, preferred_element_type=F32)


def _dot_nt(a, b):
    return lax.dot_general(a, b, (((1,), (1,)), ((), ())), preferred_element_type=F32)


def _resident(shape):
    nd = len(shape)
    return pl.BlockSpec(shape, lambda *_: (0,) * nd, pipeline_mode=pl.Buffered(1))


def _params(**kw):
    return pltpu.CompilerParams(vmem_limit_bytes=VMEM_LIMIT, **kw)


def _inproj_kernel(x_ref, gmix_ref, wa_ref, wkr_ref, gq_ref, gkv_ref, wq_ref, wqp_ref, wuk_ref, wuv_ref, cs_ref,
                   u_ref, kv_ref, kr_ref, q_ref, k_ref, v_ref):
    xn = _rms(x_ref[...], gmix_ref[...]).astype(BF16)
    u_ref[...] = _dot(xn, wa_ref[:, 0:POOL_DIM])
    ql = _rms(_dot(xn, wa_ref[:, POOL_DIM:POOL_DIM + Q_LORA]), gq_ref[...]).astype(BF16)
    kvf = _rms(_dot(xn, wa_ref[:, POOL_DIM + Q_LORA:]), gkv_ref[...])
    kv_ref[...] = kvf
    kvb = kvf.astype(BF16)
    cos = cs_ref[:, 0:LANES]
    sin = cs_ref[:, LANES:]
    hk = _dot(xn, wkr_ref[...])
    kr = hk[:, 0:LANES] * cos + hk[:, LANES:] * sin
    kr_ref[...] = kr[:, 0:ROPE_DIM]
    krb = kr.astype(BF16)
    for hp in range(N_HEADS // 2):
        qq = _dot(ql, wq_ref[:, hp * 2 * QK_DIM:(hp + 1) * 2 * QK_DIM])
        pp = _dot(ql, wqp_ref[:, hp * 2 * LANES:(hp + 1) * 2 * LANES])
        kn = _dot(kvb, wuk_ref[:, hp * 2 * NOPE_DIM:(hp + 1) * 2 * NOPE_DIM])
        vv = _dot(kvb, wuv_ref[:, hp * 2 * V_DIM:(hp + 1) * 2 * V_DIM])
        for s in range(2):
            h = 2 * hp + s
            qh = qq[:, s * QK_DIM:(s + 1) * QK_DIM]
            rope = qh[:, LANES:] * cos + pp[:, s * LANES:(s + 1) * LANES] * sin
            q_ref[h, :, 0:LANES] = qh[:, 0:LANES].astype(BF16)
            q_ref[h, :, LANES:] = rope.astype(BF16)
            k_ref[h, :, 0:LANES] = kn[:, s * NOPE_DIM:(s + 1) * NOPE_DIM].astype(BF16)
            k_ref[h, :, LANES:] = krb
            v_ref[h] = vv[:, s * V_DIM:(s + 1) * V_DIM].astype(BF16)


def _inproj(x, gmix, wa, wkr, gq, gkv, wq, wqp, wuk, wuv, cs):
    t = x.shape[0]
    tm = ROW_TILE
    row = lambda w: pl.BlockSpec((tm, w), lambda i: (i, 0))
    head = lambda w: pl.BlockSpec((N_HEADS, tm, w), lambda i: (0, i, 0))
    return pl.pallas_call(
        _inproj_kernel,
        grid=(t // tm,),
        in_specs=[row(D_MODEL), _resident(gmix.shape), _resident(wa.shape), _resident(wkr.shape),
                  _resident(gq.shape), _resident(gkv.shape), _resident(wq.shape), _resident(wqp.shape),
                  _resident(wuk.shape), _resident(wuv.shape), row(2 * LANES)],
        out_specs=[row(POOL_DIM), row(KV_LORA), row(ROPE_DIM), head(QK_DIM), head(QK_DIM), head(V_DIM)],
        out_shape=[jax.ShapeDtypeStruct((t, POOL_DIM), F32), jax.ShapeDtypeStruct((t, KV_LORA), F32),
                   jax.ShapeDtypeStruct((t, ROPE_DIM), F32), jax.ShapeDtypeStruct((N_HEADS, t, QK_DIM), BF16),
                   jax.ShapeDtypeStruct((N_HEADS, t, QK_DIM), BF16), jax.ShapeDtypeStruct((N_HEADS, t, V_DIM), BF16)],
        compiler_params=_params(),
        name="inproj",
    )(x, gmix, wa, wkr, gq, gkv, wq, wqp, wuk, wuv, cs)


def _pool_kernel(uprev_ref, u_ref, o_ref, buf_ref, *, tm, seq_len):
    i = pl.program_id(0)
    if seq_len is None:
        seq_start = i == 0
    else:
        pos0 = (i * tm) % seq_len
        seq_start = pos0 == 0
    buf_ref[0:HALO, :] = jnp.where(seq_start, 0.0, uprev_ref[...])
    buf_ref[HALO:HALO + tm, :] = u_ref[...]
    for gi, w in enumerate(POOL_WINDOWS):
        c0, c1 = gi * POOL_GROUP_DIM, (gi + 1) * POOL_GROUP_DIM
        u_new = buf_ref[HALO:HALO + tm, c0:c1]
        acc = u_new
        for j in range(1, w):
            acc = acc + buf_ref[HALO - j:HALO - j + tm, c0:c1]
        if seq_len is None:
            count = float(w)
        else:
            pos = pos0 + lax.broadcasted_iota(jnp.int32, (tm, 1), 0)
            count = jnp.minimum(w, pos + 1).astype(F32)
        o_ref[:, c0:c1] = acc / count - u_new


def _pool(u, n_rows, seq_len):
    tm = ROW_TILE
    per = tm // HALO
    return pl.pallas_call(
        functools.partial(_pool_kernel, tm=tm, seq_len=seq_len),
        grid=(n_rows // tm,),
        in_specs=[pl.BlockSpec((HALO, POOL_DIM), lambda i: (jnp.maximum(i * per - 1, 0), 0)),
                  pl.BlockSpec((tm, POOL_DIM), lambda i: (i, 0))],
        out_specs=pl.BlockSpec((tm, POOL_DIM), lambda i: (i, 0)),
        out_shape=jax.ShapeDtypeStruct((n_rows, POOL_DIM), F32),
        scratch_shapes=[pltpu.VMEM((HALO + tm, POOL_DIM), F32)],
        compiler_params=_params(),
        name="pool",
    )(u, u)


def _flash_kernel(q_ref, k_ref, v_ref, o_ref, *, tq):
    qi = pl.program_id(2)
    q = q_ref[...]

    def step(kb, carry, diagonal):
        m, l, acc = carry
        off = pl.multiple_of(kb * tq, tq)
        s = _dot_nt(q, k_ref[pl.ds(off, tq), :]) * SOFTMAX_SCALE
        if diagonal:
            r = lax.broadcasted_iota(jnp.int32, (tq, tq), 0)
            c = lax.broadcasted_iota(jnp.int32, (tq, tq), 1)
            s = jnp.where(c <= r, s, -jnp.inf)
        m_new = jnp.maximum(m, jnp.max(s, axis=-1, keepdims=True))
        alpha = jnp.exp(m - m_new)
        p = jnp.exp(s - m_new)
        l = alpha * l + jnp.sum(p, axis=-1, keepdims=True)
        acc = alpha * acc + _dot(p.astype(BF16), v_ref[pl.ds(off, tq), :])
        return m_new, l, acc

    init = (jnp.full((tq, 1), -jnp.inf, F32), jnp.zeros((tq, 1), F32), jnp.zeros((tq, V_DIM), F32))
    carry = lax.fori_loop(0, qi, lambda kb, c: step(kb, c, False), init)
    _, l, acc = step(qi, carry, True)
    o_ref[...] = acc / l


def _flash(q, k, v, n_batch, seq):
    tq = ATTN_TILE
    nq = seq // tq
    return pl.pallas_call(
        functools.partial(_flash_kernel, tq=tq),
        grid=(n_batch, N_HEADS, nq),
        in_specs=[pl.BlockSpec((None, tq, QK_DIM), lambda b, h, i: (h, b * nq + i, 0)),
                  pl.BlockSpec((None, seq, QK_DIM), lambda b, h, i: (h, b, 0)),
                  pl.BlockSpec((None, seq, V_DIM), lambda b, h, i: (h, b, 0))],
        out_specs=pl.BlockSpec((tq, V_DIM), lambda b, h, i: (b * nq + i, h)),
        out_shape=jax.ShapeDtypeStruct((n_batch * seq, N_HEADS * V_DIM), F32),
        compiler_params=_params(),
        name="flash",
    )(q, k, v)


def _qabs_kernel(q_ref, wukt_ref, qa_ref, qr_ref):
    qa_ref[...] = _dot(q_ref[:, 0:NOPE_DIM], wukt_ref[...]).astype(BF16)
    qr_ref[...] = q_ref[:, LANES:]


def _qabs(q, wukt, row0, n_rows):
    blk = row0 // n_rows
    return pl.pallas_call(
        _qabs_kernel,
        grid=(N_HEADS,),
        in_specs=[pl.BlockSpec((None, n_rows, QK_DIM), lambda h: (h, blk, 0)),
                  pl.BlockSpec((None, NOPE_DIM, KV_LORA), lambda h: (h, 0, 0))],
        out_specs=[pl.BlockSpec((n_rows, KV_LORA), lambda h: (0, h)),
                   pl.BlockSpec((n_rows, LANES), lambda h: (0, h))],
        out_shape=[jax.ShapeDtypeStruct((n_rows, N_HEADS * KV_LORA), BF16),
                   jax.ShapeDtypeStruct((n_rows, N_HEADS * LANES), BF16)],
        compiler_params=_params(),
        name="qabs",
    )(q, wukt)


def _decode_kernel(pt_ref, qa_ref, qr_ref, kvn_ref, krn_ref, *refs, pps, n_chunks, heads):
    kv_refs = refs[:pps]
    kr_refs = refs[pps:2 * pps]
    o_ref = refs[2 * pps]
    m_ref, l_ref, acc_ref, kvb_ref, krb_ref = refs[2 * pps + 1:]
    c = pl.program_id(1)

    @pl.when(c == 0)
    def _():
        m_ref[...] = jnp.full(m_ref.shape, -jnp.inf, F32)
        l_ref[...] = jnp.zeros(l_ref.shape, F32)
        acc_ref[...] = jnp.zeros(acc_ref.shape, F32)

    qa = qa_ref[...]
    qr = qr_ref[:, 0:ROPE_DIM]

    def update(s, values):
        m = m_ref[...]
        m_new = jnp.maximum(m, jnp.max(s, axis=-1, keepdims=True))
        alpha = jnp.exp(m - m_new)
        p = jnp.exp(s - m_new)
        l_ref[...] = alpha * l_ref[...] + jnp.sum(p, axis=-1, keepdims=True)
        acc_ref[...] = alpha * acc_ref[...] + _dot(p.astype(BF16), values)
        m_ref[...] = m_new

    for i in range(pps):
        kvb_ref[i * PAGE_SIZE:(i + 1) * PAGE_SIZE, :] = kv_refs[i][...].astype(BF16)
        krb_ref[i * PAGE_SIZE:(i + 1) * PAGE_SIZE, :] = kr_refs[i][...].astype(BF16)
    kvb = kvb_ref[...]
    update((_dot_nt(qa, kvb) + _dot_nt(qr, krb_ref[...])) * SOFTMAX_SCALE, kvb)

    @pl.when(c == n_chunks - 1)
    def _():
        kvn = kvn_ref[...].astype(BF16)
        s = (_dot_nt(qa, kvn) + _dot_nt(qr, krn_ref[...].astype(BF16))) * SOFTMAX_SCALE
        tok = lax.broadcasted_iota(jnp.int32, s.shape, 0) // heads
        col = lax.broadcasted_iota(jnp.int32, s.shape, 1)
        update(jnp.where(col <= tok, s, -jnp.inf), kvn)
        o_ref[...] = acc_ref[...] / l_ref[...]


def _decode(page_table, qa, qr, kvn, krn, cache_kv, cache_kr):
    bd, n_pages = page_table.shape
    rows = qa.shape[1]
    pps = PAGES_PER_STEP
    n_chunks = n_pages // pps

    def page_spec(width, i):
        return pl.BlockSpec((None, None, PAGE_SIZE, width), lambda b, c, pt: (0, pt[b, c * pps + i], 0, 0))

    per_seq = lambda r, w: pl.BlockSpec((None, r, w), lambda b, c, pt: (b, 0, 0))
    grid_spec = pltpu.PrefetchScalarGridSpec(
        num_scalar_prefetch=1,
        grid=(bd, n_chunks),
        in_specs=[per_seq(rows, KV_LORA), per_seq(rows, LANES), per_seq(NEW_ROWS_PAD, KV_LORA),
                  per_seq(NEW_ROWS_PAD, ROPE_DIM)]
                 + [page_spec(KV_LORA, i) for i in range(pps)] + [page_spec(ROPE_DIM, i) for i in range(pps)],
        out_specs=per_seq(rows, KV_LORA),
        scratch_shapes=[pltpu.VMEM((rows, 1), F32), pltpu.VMEM((rows, 1), F32), pltpu.VMEM((rows, KV_LORA), F32),
                        pltpu.VMEM((pps * PAGE_SIZE, KV_LORA), BF16), pltpu.VMEM((pps * PAGE_SIZE, ROPE_DIM), BF16)],
    )
    return pl.pallas_call(
        functools.partial(_decode_kernel, pps=pps, n_chunks=n_chunks, heads=N_HEADS),
        grid_spec=grid_spec,
        out_shape=jax.ShapeDtypeStruct((bd, rows, KV_LORA), F32),
        compiler_params=_params(),
        name="decode",
    )(page_table, qa, qr, kvn, krn, *([cache_kv] * pps), *([cache_kr] * pps))


def _ov_kernel(olat_ref, wuv_ref, o_ref):
    o_ref[...] = _dot(olat_ref[...].astype(BF16), wuv_ref[...])


def _ov(olat2d, wuv_h):
    n_rows = olat2d.shape[0]
    return pl.pallas_call(
        _ov_kernel,
        grid=(N_HEADS,),
        in_specs=[pl.BlockSpec((n_rows, KV_LORA), lambda h: (0, h)),
                  pl.BlockSpec((None, KV_LORA, V_DIM), lambda h: (h, 0, 0))],
        out_specs=pl.BlockSpec((n_rows, V_DIM), lambda h: (0, h)),
        out_shape=jax.ShapeDtypeStruct((n_rows, N_HEADS * V_DIM), F32),
        compiler_params=_params(),
        name="ov",
    )(olat2d, wuv_h)


def _post_kernel(x_ref, pooled_ref, op_ref, os_ref, gmix_ref, wg_ref, wpool_ref, pscale_ref, wo_ref, gffn_ref,
                 wrhi_ref, wrlo_ref, br_ref, x1_ref, hn_ref, ridx_ref, rw_ref, merged_ref, *, prompt_tiles):
    is_prompt = pl.program_id(0) < prompt_tiles
    x = x_ref[...]
    xn = _rms(x, gmix_ref[...]).astype(BF16)
    w = POOL_OUT_GROUP_DIM
    for g in range(len(POOL_WINDOWS)):
        cols = slice(g * w, (g + 1) * w)
        pooled = pooled_ref[:, g * POOL_GROUP_DIM:(g + 1) * POOL_GROUP_DIM].astype(BF16)
        y_pool = _dot(pooled, wpool_ref[g]) * pscale_ref[:, cols]
        gate_pool = jax.nn.sigmoid(_dot(xn, wg_ref[:, cols]))
        gate_mla = jax.nn.sigmoid(_dot(xn, wg_ref[:, D_MODEL + g * w:D_MODEL + (g + 1) * w]))
        y_mla = jnp.where(is_prompt, op_ref[:, cols], os_ref[:, cols])
        merged_ref[:, cols] = (gate_pool * y_pool + gate_mla * y_mla).astype(BF16)
    x1 = x + _dot(merged_ref[...], wo_ref[...])
    x1_ref[...] = x1
    hn = _rms(x1, gffn_ref[...])
    hn_ref[...] = hn

    hi = hn.astype(BF16)
    lo = (hn - hi.astype(F32)).astype(BF16)
    logits = _dot(hi, wrhi_ref[...]) + (_dot(hi, wrlo_ref[...]) + _dot(lo, wrhi_ref[...])) + br_ref[...]

    lane = lax.broadcasted_iota(jnp.int32, logits.shape, 1)
    lane_f = lane.astype(F32)
    none = float(LANES)
    gl = jnp.where(lane < N_GROUPS, logits, -jnp.inf)
    g_max = jnp.max(gl, axis=-1, keepdims=True)
    g_sel = jnp.min(jnp.where(gl == g_max, lane_f, none), axis=-1, keepdims=True)
    g_w = 1.0 / jnp.sum(jnp.exp(gl - g_max), axis=-1, keepdims=True)
    first = N_GROUPS + EXPERTS_PER_GROUP * g_sel
    el = jnp.where((lane_f >= first) & (lane_f < first + EXPERTS_PER_GROUP), logits, -jnp.inf)
    t1 = jnp.max(el, axis=-1, keepdims=True)
    i1 = jnp.min(jnp.where(el == t1, lane_f, none), axis=-1, keepdims=True)
    el2 = jnp.where(lane_f == i1, -jnp.inf, el)
    t2 = jnp.max(el2, axis=-1, keepdims=True)
    i2 = jnp.min(jnp.where(el2 == t2, lane_f, none), axis=-1, keepdims=True)
    e2 = jnp.exp(t2 - t1)
    w1 = g_w / (1.0 + e2)
    w2 = g_w * e2 / (1.0 + e2)
    ridx = jnp.where(lane == 0, i1 - N_GROUPS, jnp.where(lane == 1, i2 - N_GROUPS, 0.0))
    ridx_ref[...] = ridx.astype(jnp.int32)
    rw_ref[...] = jnp.where(lane == 0, w1, jnp.where(lane == 1, w2, 0.0))


def _post(x, pooled, o_prompt, o_sample, gmix, wg, wpool, pscale, wo, gffn, wrhi, wrlo, br):
    t = x.shape[0]
    tm = POST_TILE
    n_p = o_prompt.shape[0] // tm
    row = lambda w: pl.BlockSpec((tm, w), lambda i: (i, 0))
    o_p_spec = pl.BlockSpec((tm, D_MODEL), lambda i: (jnp.minimum(i, n_p - 1), 0))
    o_s_spec = pl.BlockSpec((tm, D_MODEL), lambda i: (jnp.maximum(i - n_p, 0), 0))
    return pl.pallas_call(
        functools.partial(_post_kernel, prompt_tiles=n_p),
        grid=(t // tm,),
        in_specs=[row(D_MODEL), row(POOL_DIM), o_p_spec, o_s_spec, _resident(gmix.shape), _resident(wg.shape),
                  _resident(wpool.shape), _resident(pscale.shape), _resident(wo.shape), _resident(gffn.shape),
                  _resident(wrhi.shape), _resident(wrlo.shape), _resident(br.shape)],
        out_specs=[row(D_MODEL), row(D_MODEL), row(LANES), row(LANES)],
        out_shape=[jax.ShapeDtypeStruct((t, D_MODEL), F32), jax.ShapeDtypeStruct((t, D_MODEL), F32),
                   jax.ShapeDtypeStruct((t, LANES), jnp.int32), jax.ShapeDtypeStruct((t, LANES), F32)],
        scratch_shapes=[pltpu.VMEM((tm, D_MODEL), BF16)],
        compiler_params=_params(),
        name="post",
    )(x, pooled, o_prompt, o_sample, gmix, wg, wpool, pscale, wo, gffn, wrhi, wrlo, br)


def _row_copy(src_hbm, src_row, dst_ref, dst_row, sem):
    return pltpu.make_async_copy(src_hbm.at[pl.ds(src_row, 1), :], dst_ref.at[pl.ds(dst_row, 1), :], sem)


def _gather_kernel(nbu_ref, tok_ref, hn_hbm, o_ref, sem):
    b = pl.program_id(0)
    n = o_ref.shape[0]

    @pl.when(b < nbu_ref[0])
    def _():
        def start(r, carry):
            _row_copy(hn_hbm, tok_ref[0, 0, r], o_ref, r, sem).start()
            return carry

        def wait(r, carry):
            _row_copy(hn_hbm, tok_ref[0, 0, r], o_ref, r, sem).wait()
            return carry

        lax.fori_loop(0, n, start, 0)
        lax.fori_loop(0, n, wait, 0)

    @pl.when(b >= nbu_ref[0])
    def _():
        o_ref[...] = jnp.zeros(o_ref.shape, o_ref.dtype)


def _gather(n_used, slot_tok, hn):
    nb, _, bm = slot_tok.shape
    grid_spec = pltpu.PrefetchScalarGridSpec(
        num_scalar_prefetch=1,
        grid=(nb,),
        in_specs=[pl.BlockSpec((1, 1, bm), lambda b, nbu: (b, 0, 0), memory_space=pltpu.SMEM),
                  pl.BlockSpec(memory_space=pl.ANY)],
        out_specs=pl.BlockSpec((bm, D_MODEL), lambda b, nbu: (b, 0)),
        scratch_shapes=[pltpu.SemaphoreType.DMA(())],
    )
    return pl.pallas_call(
        _gather_kernel,
        grid_spec=grid_spec,
        out_shape=jax.ShapeDtypeStruct((nb * bm, D_MODEL), hn.dtype),
        compiler_params=_params(),
        name="gather",
    )(n_used, slot_tok, hn)


def _moe_kernel(be_ref, nbu_ref, x_ref, wg_ref, wu_ref, wd_ref, o_ref, xb_ref):
    del be_ref
    b = pl.program_id(0)
    j = pl.program_id(1)

    @pl.when(b < nbu_ref[0])
    def _():
        @pl.when(j == 0)
        def _():
            xb_ref[...] = x_ref[...].astype(BF16)

        w_gate_up = jnp.concatenate([wg_ref[...].astype(BF16), wu_ref[...].astype(BF16)], axis=1)
        h = _dot(xb_ref[...], w_gate_up)
        act = (jax.nn.silu(h[:, 0:FF_CHUNK]) * h[:, FF_CHUNK:]).astype(BF16)
        part = _dot(act, wd_ref[...].astype(BF16))

        @pl.when(j == 0)
        def _():
            o_ref[...] = part

        @pl.when(j > 0)
        def _():
            o_ref[...] += part


def _moe(block_expert, n_used, xs, w_gate, w_up, w_down):
    nb = block_expert.shape[0]
    bm = xs.shape[0] // nb
    n_ch = D_EXPERT // FF_CHUNK

    def chunk(b, j, nbu):
        return jnp.where(b < nbu[0], j, n_ch - 1)

    def rows(b, j, be, nbu):
        return (jnp.minimum(b, nbu[0] - 1), 0)

    grid_spec = pltpu.PrefetchScalarGridSpec(
        num_scalar_prefetch=2,
        grid=(nb, n_ch),
        in_specs=[pl.BlockSpec((bm, D_MODEL), rows),
                  pl.BlockSpec((None, None, D_MODEL, FF_CHUNK), lambda b, j, be, nbu: (0, be[b], 0, chunk(b, j, nbu))),
                  pl.BlockSpec((None, None, D_MODEL, FF_CHUNK), lambda b, j, be, nbu: (0, be[b], 0, chunk(b, j, nbu))),
                  pl.BlockSpec((None, None, FF_CHUNK, D_MODEL), lambda b, j, be, nbu: (0, be[b], chunk(b, j, nbu), 0))],
        out_specs=pl.BlockSpec((bm, D_MODEL), rows),
        scratch_shapes=[pltpu.VMEM((bm, D_MODEL), BF16)],
    )
    return pl.pallas_call(
        _moe_kernel,
        grid_spec=grid_spec,
        out_shape=jax.ShapeDtypeStruct(xs.shape, xs.dtype),
        input_output_aliases={2: 0},
        compiler_params=_params(),
        name="moe",
    )(block_expert, n_used, xs, w_gate, w_up, w_down)


def _combine_kernel(pos_ref, x1_ref, rw_ref, gfin_ref, ys_hbm, y_ref, buf_ref, sem):
    n = x1_ref.shape[0]

    def start(r, carry):
        for k in range(TOP_K):
            _row_copy(ys_hbm, pos_ref[0, 0, TOP_K * r + k], buf_ref.at[k], r, sem).start()
        return carry

    def wait(r, carry):
        for k in range(TOP_K):
            _row_copy(ys_hbm, pos_ref[0, 0, TOP_K * r + k], buf_ref.at[k], r, sem).wait()
        return carry

    lax.fori_loop(0, n, start, 0)
    lax.fori_loop(0, n, wait, 0)
    rw = rw_ref[...]
    moe = rw[:, 0:1] * buf_ref[0] + rw[:, 1:2] * buf_ref[1]
    y_ref[...] = _rms(x1_ref[...] + moe, gfin_ref[...])


def _combine(pos, x1, rw, gfin, ys):
    t = x1.shape[0]
    tm = COMBINE_TILE
    return pl.pallas_call(
        _combine_kernel,
        grid=(t // tm,),
        in_specs=[pl.BlockSpec((1, 1, TOP_K * tm), lambda i: (i, 0, 0), memory_space=pltpu.SMEM),
                  pl.BlockSpec((tm, D_MODEL), lambda i: (i, 0)),
                  pl.BlockSpec((tm, LANES), lambda i: (i, 0)),
                  pl.BlockSpec(gfin.shape, lambda i: (0, 0)),
                  pl.BlockSpec(memory_space=pl.ANY)],
        out_specs=pl.BlockSpec((tm, D_MODEL), lambda i: (i, 0)),
        out_shape=jax.ShapeDtypeStruct((t, D_MODEL), F32),
        scratch_shapes=[pltpu.VMEM((TOP_K, tm, D_MODEL), F32), pltpu.SemaphoreType.DMA(())],
        compiler_params=_params(),
        name="combine",
    )(pos, x1, rw, gfin, ys)


def _dispatch(experts, bm):
    t = experts.shape[0]
    a = t * TOP_K
    nb = -(-(a + N_EXPERTS * (bm - 1)) // bm)
    e_flat = experts.reshape(a)
    onehot = (e_flat[:, None] == jnp.arange(N_EXPERTS, dtype=jnp.int32)[None, :]).astype(jnp.int32)
    csum = jnp.cumsum(onehot, axis=0)
    rank = jnp.sum(csum * onehot, axis=1) - 1
    counts = csum[-1]
    padded = (counts + bm - 1) // bm * bm
    pend = jnp.cumsum(padded)
    pstart = pend - padded
    dest = pstart[e_flat] + rank
    n_used = (pend[-1] // bm).astype(jnp.int32)
    blk = jnp.arange(nb, dtype=jnp.int32)
    be = jnp.minimum(jnp.searchsorted(pend, blk * bm, side='right'), N_EXPERTS - 1).astype(jnp.int32)
    be = jnp.where(blk < n_used, be, be[jnp.maximum(n_used - 1, 0)])
    slot_tok = jnp.zeros((nb * bm,), jnp.int32).at[dest].set(jnp.arange(a, dtype=jnp.int32) // TOP_K)
    return dest.astype(jnp.int32), slot_tok.reshape(nb, 1, bm), be, n_used.reshape(1)


def _rope_table(pos):
    inv = 1.0 / (ROPE_THETA ** (jnp.arange(0, ROPE_DIM, 2, dtype=F32) / ROPE_DIM))
    ang = pos.astype(F32)[:, None] * inv[None, :]
    cos, sin = jnp.cos(ang), jnp.sin(ang)
    z = jnp.zeros((pos.shape[0], LANES - ROPE_DIM), F32)
    return jnp.concatenate([cos, cos, z, sin, sin, z], axis=1)


def _rotate_half_cols(w):
    return jnp.concatenate([-w[..., HALF_ROPE:], w[..., :HALF_ROPE]], axis=-1)


def kernel(x_prompt, x_sample, state_pool, cache_kv_latent, cache_k_rope, page_table, g_mix_norm, w_in, g_q_norm,
           g_kv_norm, w_uq, w_uk, w_uv, w_pool, pool_scale, w_o, g_ffn_norm, w_group_router, b_group_router,
           w_router, b_router, w_gate_e, w_up_e, w_down_e, g_final_norm):
    n_batch, seq, d = x_prompt.shape
    bd, sd, _ = x_sample.shape
    assert w_in.shape[0] == 1 and d == D_MODEL, "single-layer step only"
    n_pages = page_table.shape[1]
    past_len = n_pages * PAGE_SIZE
    tp, ts = n_batch * seq, bd * sd
    t = tp + ts
    assert tp % ts == 0 and seq % ROW_TILE == 0 and t % ROW_TILE == 0 and ts % ROW_TILE == 0

    w_in0 = w_in[0]
    o_kr = POOL_DIM + Q_LORA + KV_LORA
    wa = w_in0[:, :o_kr].astype(BF16)
    w_kr = w_in0[:, o_kr:o_kr + ROPE_DIM]
    zk = jnp.zeros((d, LANES - ROPE_DIM), F32)
    wkr = jnp.concatenate([w_kr, zk, _rotate_half_cols(w_kr), zk], axis=1).astype(BF16)
    wg = w_in0[:, o_kr + ROPE_DIM:].astype(BF16)
    nope, rope = w_uq[0][:, :, :NOPE_DIM], w_uq[0][:, :, NOPE_DIM:]
    zq = jnp.zeros((Q_LORA, N_HEADS, LANES - ROPE_DIM), F32)
    wq = jnp.concatenate([nope, rope, zq], axis=-1).reshape(Q_LORA, N_HEADS * QK_DIM).astype(BF16)
    wqp = jnp.concatenate([_rotate_half_cols(rope), zq], axis=-1).reshape(Q_LORA, N_HEADS * LANES).astype(BF16)
    wuk = w_uk[0].reshape(KV_LORA, N_HEADS * NOPE_DIM).astype(BF16)
    wuv = w_uv[0].reshape(KV_LORA, N_HEADS * V_DIM).astype(BF16)
    wukt = w_uk[0].transpose(1, 2, 0).astype(BF16)
    wuv_h = w_uv[0].transpose(1, 0, 2).astype(BF16)
    wr = jnp.concatenate([w_group_router[0], w_router[0],
                          jnp.zeros((d, LANES - N_GROUPS - N_EXPERTS), F32)], axis=1)
    wrhi = wr.astype(BF16)
    wrlo = (wr - wrhi.astype(F32)).astype(BF16)
    br = jnp.concatenate([b_group_router[0], b_router[0],
                          jnp.zeros((LANES - N_GROUPS - N_EXPERTS,), F32)]).reshape(1, LANES)
    row = lambda g: g.reshape(1, -1)

    pos = jnp.concatenate([jnp.tile(jnp.arange(seq, dtype=jnp.int32), n_batch),
                           past_len + jnp.tile(jnp.arange(sd, dtype=jnp.int32), bd)])
    cs = _rope_table(pos)

    x = jnp.concatenate([x_prompt.reshape(tp, d), x_sample.reshape(ts, d)], axis=0)
    u, kv, kr, q, k, v = _inproj(x, row(g_mix_norm[0]), wa, wkr, row(g_q_norm[0]), row(g_kv_norm[0]),
                                 wq, wqp, wuk, wuv, cs)

    u_s = u[tp:].reshape(bd, sd, POOL_DIM)
    slab = HALO + 8
    u_full_s = jnp.concatenate([state_pool[0].astype(F32), u_s,
                                jnp.zeros((bd, slab - POOL_STATE_LEN - sd, POOL_DIM), F32)], axis=1)
    pooled_p = _pool(u, tp, seq)
    pooled_s = _pool(u_full_s.reshape(bd * slab, POOL_DIM), bd * slab, None)
    pooled_s = pooled_s.reshape(bd, slab, POOL_DIM)[:, POOL_STATE_LEN:POOL_STATE_LEN + sd].reshape(ts, POOL_DIM)
    pooled = jnp.concatenate([pooled_p, pooled_s], axis=0)

    o_prompt = _flash(q, k, v, n_batch, seq)
    qa, qr = _qabs(q, wukt, tp, ts)
    qa = qa.reshape(bd, sd * N_HEADS, KV_LORA)
    qr = qr.reshape(bd, sd * N_HEADS, LANES)
    pad_new = lambda z: jnp.pad(z.reshape(bd, sd, -1), ((0, 0), (0, NEW_ROWS_PAD - sd), (0, 0)))
    o_lat = _decode(page_table, qa, qr, pad_new(kv[tp:]), pad_new(kr[tp:]), cache_kv_latent, cache_k_rope)
    o_sample = _ov(o_lat.reshape(ts, N_HEADS * KV_LORA), wuv_h)

    x1, hn, ridx, rw = _post(x, pooled, o_prompt, o_sample, row(g_mix_norm[0]), wg, w_pool[0].astype(BF16), row(pool_scale[0]),
                             w_o[0].astype(BF16), row(g_ffn_norm[0]), wrhi, wrlo, br)

    dest, slot_tok, block_expert, n_used = _dispatch(ridx[:, :TOP_K], MOE_BLOCK)
    xs = _gather(n_used, slot_tok, hn)
    ys = _moe(block_expert, n_used, xs, w_gate_e, w_up_e, w_down_e)
    y = _combine(dest.reshape(t // COMBINE_TILE, 1, TOP_K * COMBINE_TILE), x1, rw, row(g_final_norm), ys)

    u_p = u[:tp].reshape(n_batch, seq, POOL_DIM)
    pool_s = jnp.concatenate([state_pool[0].astype(F32), u_s], axis=1)[:, -POOL_STATE_LEN:]
    return (y[:tp].reshape(n_batch, seq, d), y[tp:].reshape(bd, sd, d),
            kv[:tp].reshape(1, n_batch, seq, KV_LORA), kr[:tp].reshape(1, n_batch, seq, ROPE_DIM),
            u_p[:, -POOL_STATE_LEN:][None],
            kv[tp:].reshape(1, bd, sd, KV_LORA), kr[tp:].reshape(1, bd, sd, ROPE_DIM), pool_s[None])
```

```python
import functools

import jax
import jax.numpy as jnp
from jax import lax
from jax.experimental import pallas as pl
from jax.experimental.pallas import tpu as pltpu

F32 = jnp.float32
BF16 = jnp.bfloat16

D_MODEL = 2048
POOL_WINDOWS = (2, 4, 8, 16)
POOL_GROUP_DIM = 256
POOL_DIM = 1024
POOL_OUT_GROUP_DIM = 512
POOL_STATE_LEN = 15
N_HEADS = 16
V_DIM = 128
Q_LORA = 512
KV_LORA = 512
NOPE_DIM = 128
ROPE_DIM = 64
HALF_ROPE = ROPE_DIM // 2
ROPE_THETA = 10000.0
SOFTMAX_SCALE = (NOPE_DIM + ROPE_DIM) ** -0.5
SCORE_SCALE_LOG2 = SOFTMAX_SCALE * 1.4426950408889634
PAGE_SIZE = 128
N_GROUPS = 8
EXPERTS_PER_GROUP = 8
N_EXPERTS = 64
TOP_K = 2
D_EXPERT = 1408
EPS = 1e-6

LANES = 128
QK_DIM = 2 * LANES
HALO = 16
ROW_TILE = 256
POST_TILE = 128
ATTN_TILE = 512
DECODE_STREAMS = 2
PAGES_PER_STREAM = 8
NEW_ROWS_PAD = 16
MOE_BLOCK = 384
FF_CHUNK = 256
EXPERT_OUT_CHUNK = 512
COMBINE_TILE = 128
VMEM_LIMIT = 56 * 1024 * 1024


def _rms(xf, g):
    ms = jnp.mean(xf * xf, axis=-1, keepdims=True)
    return xf * lax.rsqrt(ms + EPS) * g


def _dot(a, b):
    return jnp.dot(a, b, preferred_element_type=F32)


def _dot_nt(a, b):
    return lax.dot_general(a, b, (((1,), (1,)), ((), ())), preferred_element_type=F32)


def _resident(shape):
    nd = len(shape)
    return pl.BlockSpec(shape, lambda *_: (0,) * nd, pipeline_mode=pl.Buffered(1))


def _params(**kw):
    return pltpu.CompilerParams(vmem_limit_bytes=VMEM_LIMIT, **kw)


def _split_rows(tm, width, n_first):
    return (pl.BlockSpec((tm, width), lambda i: (jnp.minimum(i, n_first - 1), 0)),
            pl.BlockSpec((tm, width), lambda i: (jnp.maximum(i - n_first, 0), 0)))


def _inproj_kernel(x_ref, xs_ref, gmix_ref, wa_ref, wkr_ref, gq_ref, gkv_ref, wq_ref, wqp_ref, wuk_ref, wuv_ref,
                   cs_ref, u_ref, kv_ref, kr_ref, q_ref, k_ref, v_ref, *, prompt_tiles):
    x = jnp.where(pl.program_id(0) < prompt_tiles, x_ref[...], xs_ref[...])
    xn = _rms(x, gmix_ref[...]).astype(BF16)
    u_ref[...] = _dot(xn, wa_ref[:, 0:POOL_DIM])
    ql = _rms(_dot(xn, wa_ref[:, POOL_DIM:POOL_DIM + Q_LORA]), gq_ref[...]).astype(BF16)
    kvf = _rms(_dot(xn, wa_ref[:, POOL_DIM + Q_LORA:]), gkv_ref[...])
    kv_ref[...] = kvf
    kvb = kvf.astype(BF16)
    cos = cs_ref[:, 0:LANES]
    sin = cs_ref[:, LANES:]
    hk = _dot(xn, wkr_ref[...])
    kr = hk[:, 0:LANES] * cos + hk[:, LANES:] * sin
    kr_ref[...] = kr[:, 0:ROPE_DIM]
    krb = kr.astype(BF16)
    for hp in range(N_HEADS // 2):
        qq = _dot(ql, wq_ref[:, hp * 2 * QK_DIM:(hp + 1) * 2 * QK_DIM])
        pp = _dot(ql, wqp_ref[:, hp * 2 * LANES:(hp + 1) * 2 * LANES])
        kn = _dot(kvb, wuk_ref[:, hp * 2 * NOPE_DIM:(hp + 1) * 2 * NOPE_DIM])
        vv = _dot(kvb, wuv_ref[:, hp * 2 * V_DIM:(hp + 1) * 2 * V_DIM])
        for s in range(2):
            h = 2 * hp + s
            qh = qq[:, s * QK_DIM:(s + 1) * QK_DIM]
            rope = qh[:, LANES:] * cos + pp[:, s * LANES:(s + 1) * LANES] * sin
            q_ref[h, :, 0:LANES] = qh[:, 0:LANES].astype(BF16)
            q_ref[h, :, LANES:] = rope.astype(BF16)
            k_ref[h, :, 0:LANES] = kn[:, s * NOPE_DIM:(s + 1) * NOPE_DIM].astype(BF16)
            k_ref[h, :, LANES:] = krb
            v_ref[h] = vv[:, s * V_DIM:(s + 1) * V_DIM].astype(BF16)


def _inproj(x_p, x_s, gmix, wa, wkr, gq, gkv, wq, wqp, wuk, wuv, cs):
    tm = ROW_TILE
    n_p = x_p.shape[0] // tm
    t = x_p.shape[0] + x_s.shape[0]
    row = lambda w: pl.BlockSpec((tm, w), lambda i: (i, 0))
    head = lambda w: pl.BlockSpec((N_HEADS, tm, w), lambda i: (0, i, 0))
    return pl.pallas_call(
        functools.partial(_inproj_kernel, prompt_tiles=n_p),
        grid=(t // tm,),
        in_specs=[*_split_rows(tm, D_MODEL, n_p), _resident(gmix.shape), _resident(wa.shape), _resident(wkr.shape),
                  _resident(gq.shape), _resident(gkv.shape), _resident(wq.shape), _resident(wqp.shape),
                  _resident(wuk.shape), _resident(wuv.shape), row(2 * LANES)],
        out_specs=[row(POOL_DIM), row(KV_LORA), row(ROPE_DIM), head(QK_DIM), head(QK_DIM), head(V_DIM)],
        out_shape=[jax.ShapeDtypeStruct((t, POOL_DIM), F32), jax.ShapeDtypeStruct((t, KV_LORA), F32),
                   jax.ShapeDtypeStruct((t, ROPE_DIM), F32), jax.ShapeDtypeStruct((N_HEADS, t, QK_DIM), BF16),
                   jax.ShapeDtypeStruct((N_HEADS, t, QK_DIM), BF16), jax.ShapeDtypeStruct((N_HEADS, t, V_DIM), BF16)],
        compiler_params=_params(),
        name="inproj",
    )(x_p, x_s, gmix, wa, wkr, gq, gkv, wq, wqp, wuk, wuv, cs)


def _pool_kernel(uprev_ref, u_ref, o_ref, buf_ref, *, tm, seq_len):
    i = pl.program_id(0)
    if seq_len is None:
        seq_start = i == 0
    else:
        pos0 = (i * tm) % seq_len
        seq_start = pos0 == 0
    buf_ref[0:HALO, :] = jnp.where(seq_start, 0.0, uprev_ref[...])
    buf_ref[HALO:HALO + tm, :] = u_ref[...]
    for gi, w in enumerate(POOL_WINDOWS):
        c0, c1 = gi * POOL_GROUP_DIM, (gi + 1) * POOL_GROUP_DIM
        u_new = buf_ref[HALO:HALO + tm, c0:c1]
        acc = u_new
        for j in range(1, w):
            acc = acc + buf_ref[HALO - j:HALO - j + tm, c0:c1]
        if seq_len is None:
            count = float(w)
        else:
            pos = pos0 + lax.broadcasted_iota(jnp.int32, (tm, 1), 0)
            count = jnp.minimum(w, pos + 1).astype(F32)
        o_ref[:, c0:c1] = acc / count - u_new


def _pool(u, n_rows, seq_len):
    tm = ROW_TILE
    per = tm // HALO
    return pl.pallas_call(
        functools.partial(_pool_kernel, tm=tm, seq_len=seq_len),
        grid=(n_rows // tm,),
        in_specs=[pl.BlockSpec((HALO, POOL_DIM), lambda i: (jnp.maximum(i * per - 1, 0), 0)),
                  pl.BlockSpec((tm, POOL_DIM), lambda i: (i, 0))],
        out_specs=pl.BlockSpec((tm, POOL_DIM), lambda i: (i, 0)),
        out_shape=jax.ShapeDtypeStruct((n_rows, POOL_DIM), F32),
        scratch_shapes=[pltpu.VMEM((HALO + tm, POOL_DIM), F32)],
        compiler_params=_params(),
        name="pool",
    )(u, u)


def _flash_kernel(q_ref, k_ref, v_ref, o_ref, *, tq):
    qi = pl.program_id(2)
    q = q_ref[...]

    def step(kb, carry, diagonal):
        m, l, acc = carry
        off = pl.multiple_of(kb * tq, tq)
        s = _dot_nt(q, k_ref[pl.ds(off, tq), :]) * SCORE_SCALE_LOG2
        if diagonal:
            r = lax.broadcasted_iota(jnp.int32, (tq, tq), 0)
            c = lax.broadcasted_iota(jnp.int32, (tq, tq), 1)
            s = jnp.where(c <= r, s, -jnp.inf)
        m_new = jnp.maximum(m, jnp.max(s, axis=-1, keepdims=True))
        alpha = jnp.exp2(m - m_new)
        p = jnp.exp2(s - m_new)
        l = alpha * l + jnp.sum(p, axis=-1, keepdims=True)
        acc = alpha * acc + _dot(p.astype(BF16), v_ref[pl.ds(off, tq), :])
        return m_new, l, acc

    init = (jnp.full((tq, 1), -jnp.inf, F32), jnp.zeros((tq, 1), F32), jnp.zeros((tq, V_DIM), F32))
    carry = lax.fori_loop(0, qi, lambda kb, c: step(kb, c, False), init)
    _, l, acc = step(qi, carry, True)
    o_ref[...] = acc / l


def _flash(q, k, v, n_batch, seq):
    tq = ATTN_TILE
    nq = seq // tq
    return pl.pallas_call(
        functools.partial(_flash_kernel, tq=tq),
        grid=(n_batch, N_HEADS, nq),
        in_specs=[pl.BlockSpec((None, tq, QK_DIM), lambda b, h, i: (h, b * nq + i, 0)),
                  pl.BlockSpec((None, seq, QK_DIM), lambda b, h, i: (h, b, 0)),
                  pl.BlockSpec((None, seq, V_DIM), lambda b, h, i: (h, b, 0))],
        out_specs=pl.BlockSpec((tq, V_DIM), lambda b, h, i: (b * nq + i, h)),
        out_shape=jax.ShapeDtypeStruct((n_batch * seq, N_HEADS * V_DIM), F32),
        compiler_params=_params(),
        name="flash",
    )(q, k, v)


def _qabs_kernel(q_ref, wukt_ref, qa_ref, qr_ref):
    qa_ref[...] = _dot(q_ref[:, 0:NOPE_DIM], wukt_ref[...]).astype(BF16)
    qr_ref[...] = q_ref[:, LANES:]


def _qabs(q, wukt, row0, n_rows):
    blk = row0 // n_rows
    return pl.pallas_call(
        _qabs_kernel,
        grid=(N_HEADS,),
        in_specs=[pl.BlockSpec((None, n_rows, QK_DIM), lambda h: (h, blk, 0)),
                  pl.BlockSpec((None, NOPE_DIM, KV_LORA), lambda h: (h, 0, 0))],
        out_specs=[pl.BlockSpec((n_rows, KV_LORA), lambda h: (0, h)),
                   pl.BlockSpec((n_rows, LANES), lambda h: (0, h))],
        out_shape=[jax.ShapeDtypeStruct((n_rows, N_HEADS * KV_LORA), BF16),
                   jax.ShapeDtypeStruct((n_rows, N_HEADS * LANES), BF16)],
        compiler_params=_params(),
        name="qabs",
    )(q, wukt)


def _decode_kernel(pt_ref, qa_ref, qr_ref, kvn_ref, krn_ref, kv_hbm, krt_hbm, o_ref,
                   kv_buf, krt_buf, m_ref, l_ref, acc_ref, kv_sem, kr_sem, *, n_chunks, heads):
    b = pl.program_id(0)
    n_seq = pl.num_programs(0)
    cp = kv_buf.shape[1]
    per_stream = cp // DECODE_STREAMS

    def page_copies(seq, chunk, slot, i):
        page = pt_ref[seq, chunk * cp + i]
        return (pltpu.make_async_copy(kv_hbm.at[0, page], kv_buf.at[slot, i], kv_sem.at[slot]),
                pltpu.make_async_copy(krt_hbm.at[0, page], krt_buf.at[slot, i], kr_sem.at[slot]))

    def start_chunk(seq, chunk, slot):
        for i in range(cp):
            for copy in page_copies(seq, chunk, slot, i):
                copy.start()

    def wait_chunk(seq, chunk, slot):
        for i in range(cp):
            for copy in page_copies(seq, chunk, slot, i):
                copy.wait()

    @pl.when(b == 0)
    def _():
        start_chunk(0, 0, 0)

    m_ref[...] = jnp.full(m_ref.shape, -jnp.inf, F32)
    l_ref[...] = jnp.zeros(l_ref.shape, F32)
    acc_ref[...] = jnp.zeros(acc_ref.shape, F32)
    qa = qa_ref[...]
    qr = qr_ref[:, 0:ROPE_DIM]

    def consume(slot):
        for t in range(DECODE_STREAMS):
            p0 = t * per_stream
            kvb = kv_buf[slot, p0:p0 + per_stream].reshape(per_stream * PAGE_SIZE, KV_LORA).astype(BF16)
            krt = jnp.concatenate([krt_buf[slot, p0 + i] for i in range(per_stream)], axis=1).astype(BF16)
            s = (_dot_nt(qa, kvb) + _dot(qr, krt)) * SCORE_SCALE_LOG2
            m = m_ref[t]
            m_new = jnp.maximum(m, jnp.max(s, axis=-1, keepdims=True))
            alpha = jnp.exp2(m - m_new)
            p = jnp.exp2(s - m_new)
            l_ref[t] = alpha * l_ref[t] + jnp.sum(p, axis=-1, keepdims=True)
            acc_ref[t] = alpha * acc_ref[t] + _dot(p.astype(BF16), kvb)
            m_ref[t] = m_new

    def chunk_pair(cc, carry):
        c0 = 2 * cc
        wait_chunk(b, c0, 0)
        start_chunk(b, c0 + 1, 1)
        consume(0)
        wait_chunk(b, c0 + 1, 1)

        @pl.when(c0 + 2 < n_chunks)
        def _():
            start_chunk(b, c0 + 2, 0)

        @pl.when((c0 + 2 == n_chunks) & (b + 1 < n_seq))
        def _():
            start_chunk(b + 1, 0, 0)

        consume(1)
        return carry

    lax.fori_loop(0, n_chunks // 2, chunk_pair, 0)

    kvn = kvn_ref[...].astype(BF16)
    s_new = (_dot_nt(qa, kvn) + _dot_nt(qr, krn_ref[...].astype(BF16))) * SCORE_SCALE_LOG2
    tok = lax.broadcasted_iota(jnp.int32, s_new.shape, 0) // heads
    col = lax.broadcasted_iota(jnp.int32, s_new.shape, 1)
    s_new = jnp.where(col <= tok, s_new, -jnp.inf)
    m = jnp.max(s_new, axis=-1, keepdims=True)
    for t in range(DECODE_STREAMS):
        m = jnp.maximum(m, m_ref[t])
    p_new = jnp.exp2(s_new - m)
    l = jnp.sum(p_new, axis=-1, keepdims=True)
    acc = _dot(p_new.astype(BF16), kvn)
    for t in range(DECODE_STREAMS):
        w = jnp.exp2(m_ref[t] - m)
        l = l + w * l_ref[t]
        acc = acc + w * acc_ref[t]
    o_ref[...] = acc / l


def _decode(page_table, qa, qr, kvn, krn, cache_kv, cache_krt):
    bd, n_pages = page_table.shape
    rows = qa.shape[1]
    cp = DECODE_STREAMS * PAGES_PER_STREAM
    n_chunks = n_pages // cp
    assert n_pages % cp == 0 and n_chunks % 2 == 0
    per_seq = lambda r, w: pl.BlockSpec((None, r, w), lambda b, pt: (b, 0, 0))
    grid_spec = pltpu.PrefetchScalarGridSpec(
        num_scalar_prefetch=1,
        grid=(bd,),
        in_specs=[per_seq(rows, KV_LORA), per_seq(rows, LANES), per_seq(NEW_ROWS_PAD, KV_LORA),
                  per_seq(NEW_ROWS_PAD, ROPE_DIM), pl.BlockSpec(memory_space=pl.ANY),
                  pl.BlockSpec(memory_space=pl.ANY)],
        out_specs=per_seq(rows, KV_LORA),
        scratch_shapes=[pltpu.VMEM((2, cp, PAGE_SIZE, KV_LORA), F32), pltpu.VMEM((2, cp, ROPE_DIM, PAGE_SIZE), F32),
                        pltpu.VMEM((DECODE_STREAMS, rows, 1), F32), pltpu.VMEM((DECODE_STREAMS, rows, 1), F32),
                        pltpu.VMEM((DECODE_STREAMS, rows, KV_LORA), F32),
                        pltpu.SemaphoreType.DMA((2,)), pltpu.SemaphoreType.DMA((2,))],
    )
    return pl.pallas_call(
        functools.partial(_decode_kernel, n_chunks=n_chunks, heads=N_HEADS),
        grid_spec=grid_spec,
        out_shape=jax.ShapeDtypeStruct((bd, rows, KV_LORA), F32),
        compiler_params=_params(),
        name="decode",
    )(page_table, qa, qr, kvn, krn, cache_kv, cache_krt)


def _ov_kernel(olat_ref, wuv_ref, o_ref):
    o_ref[...] = _dot(olat_ref[...].astype(BF16), wuv_ref[...])


def _ov(olat2d, wuv_h):
    n_rows = olat2d.shape[0]
    return pl.pallas_call(
        _ov_kernel,
        grid=(N_HEADS,),
        in_specs=[pl.BlockSpec((n_rows, KV_LORA), lambda h: (0, h)),
                  pl.BlockSpec((None, KV_LORA, V_DIM), lambda h: (h, 0, 0))],
        out_specs=pl.BlockSpec((n_rows, V_DIM), lambda h: (0, h)),
        out_shape=jax.ShapeDtypeStruct((n_rows, N_HEADS * V_DIM), F32),
        compiler_params=_params(),
        name="ov",
    )(olat2d, wuv_h)


def _post_kernel(xp_ref, xs_ref, pp_ref, ps_ref, op_ref, os_ref, gmix_ref, wg_ref, wpool_ref, pscale_ref, wo_ref,
                 gffn_ref, wrhi_ref, wrlo_ref, br_ref, x1_ref, hn_ref, ridx_ref, rw_ref, merged_ref, *,
                 prompt_tiles):
    is_prompt = pl.program_id(0) < prompt_tiles
    x = jnp.where(is_prompt, xp_ref[...], xs_ref[...])
    xn = _rms(x, gmix_ref[...]).astype(BF16)
    w = POOL_OUT_GROUP_DIM
    for g in range(len(POOL_WINDOWS)):
        cols = slice(g * w, (g + 1) * w)
        pcols = slice(g * POOL_GROUP_DIM, (g + 1) * POOL_GROUP_DIM)
        pooled = jnp.where(is_prompt, pp_ref[:, pcols], ps_ref[:, pcols]).astype(BF16)
        y_pool = _dot(pooled, wpool_ref[g]) * pscale_ref[:, cols]
        gate_pool = jax.nn.sigmoid(_dot(xn, wg_ref[:, cols]))
        gate_mla = jax.nn.sigmoid(_dot(xn, wg_ref[:, D_MODEL + g * w:D_MODEL + (g + 1) * w]))
        y_mla = jnp.where(is_prompt, op_ref[:, cols], os_ref[:, cols])
        merged_ref[:, cols] = (gate_pool * y_pool + gate_mla * y_mla).astype(BF16)
    x1 = x + _dot(merged_ref[...], wo_ref[...])
    x1_ref[...] = x1
    hn = _rms(x1, gffn_ref[...])
    hn_ref[...] = hn

    hi = hn.astype(BF16)
    lo = (hn - hi.astype(F32)).astype(BF16)
    logits = _dot(hi, wrhi_ref[...]) + (_dot(hi, wrlo_ref[...]) + _dot(lo, wrhi_ref[...])) + br_ref[...]

    lane = lax.broadcasted_iota(jnp.int32, logits.shape, 1)
    lane_f = lane.astype(F32)
    none = float(LANES)
    gl = jnp.where(lane < N_GROUPS, logits, -jnp.inf)
    g_max = jnp.max(gl, axis=-1, keepdims=True)
    g_sel = jnp.min(jnp.where(gl == g_max, lane_f, none), axis=-1, keepdims=True)
    g_w = 1.0 / jnp.sum(jnp.exp(gl - g_max), axis=-1, keepdims=True)
    first = N_GROUPS + EXPERTS_PER_GROUP * g_sel
    el = jnp.where((lane_f >= first) & (lane_f < first + EXPERTS_PER_GROUP), logits, -jnp.inf)
    t1 = jnp.max(el, axis=-1, keepdims=True)
    i1 = jnp.min(jnp.where(el == t1, lane_f, none), axis=-1, keepdims=True)
    el2 = jnp.where(lane_f == i1, -jnp.inf, el)
    t2 = jnp.max(el2, axis=-1, keepdims=True)
    i2 = jnp.min(jnp.where(el2 == t2, lane_f, none), axis=-1, keepdims=True)
    e2 = jnp.exp(t2 - t1)
    w1 = g_w / (1.0 + e2)
    w2 = g_w * e2 / (1.0 + e2)
    ridx = jnp.where(lane == 0, i1 - N_GROUPS, jnp.where(lane == 1, i2 - N_GROUPS, 0.0))
    ridx_ref[...] = ridx.astype(jnp.int32)
    rw_ref[...] = jnp.where(lane == 0, w1, jnp.where(lane == 1, w2, 0.0))


def _post(x_p, x_s, pooled_p, pooled_s, o_prompt, o_sample, gmix, wg, wpool, pscale, wo, gffn, wrhi, wrlo, br):
    t = x_p.shape[0] + x_s.shape[0]
    tm = POST_TILE
    n_p = x_p.shape[0] // tm
    row = lambda w: pl.BlockSpec((tm, w), lambda i: (i, 0))
    return pl.pallas_call(
        functools.partial(_post_kernel, prompt_tiles=n_p),
        grid=(t // tm,),
        in_specs=[*_split_rows(tm, D_MODEL, n_p), *_split_rows(tm, POOL_DIM, n_p), *_split_rows(tm, D_MODEL, n_p),
                  _resident(gmix.shape), _resident(wg.shape),
                  _resident(wpool.shape), _resident(pscale.shape), _resident(wo.shape), _resident(gffn.shape),
                  _resident(wrhi.shape), _resident(wrlo.shape), _resident(br.shape)],
        out_specs=[row(D_MODEL), row(D_MODEL), row(LANES), row(LANES)],
        out_shape=[jax.ShapeDtypeStruct((t, D_MODEL), F32), jax.ShapeDtypeStruct((t, D_MODEL), F32),
                   jax.ShapeDtypeStruct((t, LANES), jnp.int32), jax.ShapeDtypeStruct((t, LANES), F32)],
        scratch_shapes=[pltpu.VMEM((tm, D_MODEL), BF16)],
        compiler_params=_params(),
        name="post",
    )(x_p, x_s, pooled_p, pooled_s, o_prompt, o_sample, gmix, wg, wpool, pscale, wo, gffn, wrhi, wrlo, br)


def _row_copy(src_hbm, src_row, dst_ref, dst_row, sem):
    return pltpu.make_async_copy(src_hbm.at[pl.ds(src_row, 1), :], dst_ref.at[pl.ds(dst_row, 1), :], sem)


def _moe_kernel(be_ref, nbu_ref, tok_next_ref, tok_first_ref, hn_hbm, wg_hbm, wu_hbm, wd_hbm, o_ref,
                x_buf, act_ref, gu_buf, d_buf, x_sem, g_sem, u_sem, d_sem):
    b = pl.program_id(0)
    n_used = nbu_ref[0]
    bm = x_buf.shape[1]
    ff_stages = [(c0, min(FF_CHUNK, D_EXPERT - c0)) for c0 in range(0, D_EXPERT, FF_CHUNK)]
    n_gu = len(ff_stages)
    n_dn = D_MODEL // EXPERT_OUT_CHUNK
    assert n_gu % 2 == 0 and n_dn % 2 == 0

    def gate_up_copies(e, j, slot):
        c0, width = ff_stages[j]
        cols = pl.ds(c0, width)
        return (pltpu.make_async_copy(wg_hbm.at[0, e, :, cols], gu_buf.at[slot, 0, :, pl.ds(0, width)], g_sem.at[slot]),
                pltpu.make_async_copy(wu_hbm.at[0, e, :, cols], gu_buf.at[slot, 1, :, pl.ds(0, width)], u_sem.at[slot]))

    def down_copy(e, c, slot):
        cols = pl.ds(c * EXPERT_OUT_CHUNK, EXPERT_OUT_CHUNK)
        return pltpu.make_async_copy(wd_hbm.at[0, e, :, cols], d_buf.at[slot], d_sem.at[slot])

    def start_gate_up(e, j, slot):
        for copy in gate_up_copies(e, j, slot):
            copy.start()

    def start_rows(tok_ref, slot):
        def body(r, carry):
            _row_copy(hn_hbm, tok_ref[0, 0, r], x_buf.at[slot], r, x_sem.at[slot]).start()
            return carry
        lax.fori_loop(0, bm, body, 0)

    def wait_rows(slot):
        def body(r, carry):
            _row_copy(hn_hbm, 0, x_buf.at[slot], r, x_sem.at[slot]).wait()
            return carry
        lax.fori_loop(0, bm, body, 0)

    @pl.when(b == 0)
    def _():
        start_gate_up(be_ref[0], 0, 0)
        start_gate_up(be_ref[0], 1, 1)
        start_rows(tok_first_ref, 0)

    @pl.when(b < n_used)
    def _():
        e = be_ref[b]
        x_slot = b % 2
        down_copy(e, 0, 0).start()
        down_copy(e, 1, 1).start()

        @pl.when(b + 1 < n_used)
        def _():
            start_rows(tok_next_ref, 1 - x_slot)

        wait_rows(x_slot)
        xb = x_buf[x_slot].astype(BF16)

        def fetch_gate_up(j):
            slot = j % 2
            width = ff_stages[j][1]
            for copy in gate_up_copies(e, j, slot):
                copy.wait()
            w = jnp.concatenate([gu_buf[slot, 0, :, 0:width].astype(BF16),
                                 gu_buf[slot, 1, :, 0:width].astype(BF16)], axis=1)
            if j + 2 < n_gu:
                start_gate_up(e, j + 2, slot)
            else:
                @pl.when(b + 1 < n_used)
                def _():
                    start_gate_up(be_ref[b + 1], j + 2 - n_gu, slot)
            return w

        def fetch_down(c):
            slot = c % 2
            down_copy(e, c, slot).wait()
            w = d_buf[slot].astype(BF16)
            if c + 2 < n_dn:
                down_copy(e, c + 2, slot).start()
            return w

        w = fetch_gate_up(0)
        for j in range(n_gu):
            w_next = fetch_gate_up(j + 1) if j + 1 < n_gu else fetch_down(0)
            c0, width = ff_stages[j]
            h = _dot(xb, w)
            act = jax.nn.silu(h[:, 0:width]) * h[:, width:]
            act_ref[:, c0:c0 + width] = act.astype(BF16)
            w = w_next
        act_all = act_ref[...]
        for c in range(n_dn):
            w_next = fetch_down(c + 1) if c + 1 < n_dn else None
            o_ref[:, c * EXPERT_OUT_CHUNK:(c + 1) * EXPERT_OUT_CHUNK] = _dot(act_all, w)
            w = w_next

    @pl.when(b >= n_used)
    def _():
        o_ref[...] = jnp.zeros(o_ref.shape, o_ref.dtype)


def _moe(block_expert, n_used, slot_tok, hn, w_gate, w_up, w_down):
    nb, _, bm = slot_tok.shape
    tok_spec = lambda index_map: pl.BlockSpec((1, 1, bm), index_map, memory_space=pltpu.SMEM)
    grid_spec = pltpu.PrefetchScalarGridSpec(
        num_scalar_prefetch=2,
        grid=(nb,),
        in_specs=[tok_spec(lambda b, be, nbu: (jnp.minimum(b + 1, nb - 1), 0, 0)),
                  tok_spec(lambda b, be, nbu: (0, 0, 0)),
                  pl.BlockSpec(memory_space=pl.ANY), pl.BlockSpec(memory_space=pl.ANY),
                  pl.BlockSpec(memory_space=pl.ANY), pl.BlockSpec(memory_space=pl.ANY)],
        out_specs=pl.BlockSpec((bm, D_MODEL), lambda b, be, nbu: (b, 0)),
        scratch_shapes=[pltpu.VMEM((2, bm, D_MODEL), F32), pltpu.VMEM((bm, D_EXPERT), BF16),
                        pltpu.VMEM((2, 2, D_MODEL, FF_CHUNK), F32), pltpu.VMEM((2, D_EXPERT, EXPERT_OUT_CHUNK), F32),
                        pltpu.SemaphoreType.DMA((2,)), pltpu.SemaphoreType.DMA((2,)),
                        pltpu.SemaphoreType.DMA((2,)), pltpu.SemaphoreType.DMA((2,))],
    )
    return pl.pallas_call(
        _moe_kernel,
        grid_spec=grid_spec,
        out_shape=jax.ShapeDtypeStruct((nb * bm, D_MODEL), F32),
        compiler_params=_params(),
        name="moe",
    )(block_expert, n_used, slot_tok, slot_tok, hn, w_gate, w_up, w_down)


def _combine_kernel(pos_ref, x1_ref, rw_ref, gfin_ref, ys_hbm, yp_ref, ys_ref, buf_ref, sem, *, prompt_tiles):
    n = x1_ref.shape[0]

    def start(r, carry):
        for k in range(TOP_K):
            _row_copy(ys_hbm, pos_ref[0, 0, TOP_K * r + k], buf_ref.at[k], r, sem).start()
        return carry

    def wait(r, carry):
        for k in range(TOP_K):
            _row_copy(ys_hbm, pos_ref[0, 0, TOP_K * r + k], buf_ref.at[k], r, sem).wait()
        return carry

    lax.fori_loop(0, n, start, 0)
    lax.fori_loop(0, n, wait, 0)
    rw = rw_ref[...]
    moe = rw[:, 0:1] * buf_ref[0] + rw[:, 1:2] * buf_ref[1]
    y = _rms(x1_ref[...] + moe, gfin_ref[...])
    is_prompt = pl.program_id(0) < prompt_tiles

    @pl.when(is_prompt)
    def _():
        yp_ref[...] = y

    @pl.when(jnp.logical_not(is_prompt))
    def _():
        ys_ref[...] = y


def _combine(pos, x1, rw, gfin, ys, n_prompt_rows):
    t = x1.shape[0]
    tm = COMBINE_TILE
    n_p = n_prompt_rows // tm
    return pl.pallas_call(
        functools.partial(_combine_kernel, prompt_tiles=n_p),
        grid=(t // tm,),
        in_specs=[pl.BlockSpec((1, 1, TOP_K * tm), lambda i: (i, 0, 0), memory_space=pltpu.SMEM),
                  pl.BlockSpec((tm, D_MODEL), lambda i: (i, 0)),
                  pl.BlockSpec((tm, LANES), lambda i: (i, 0)),
                  pl.BlockSpec(gfin.shape, lambda i: (0, 0)),
                  pl.BlockSpec(memory_space=pl.ANY)],
        out_specs=list(_split_rows(tm, D_MODEL, n_p)),
        out_shape=[jax.ShapeDtypeStruct((n_prompt_rows, D_MODEL), F32),
                   jax.ShapeDtypeStruct((t - n_prompt_rows, D_MODEL), F32)],
        scratch_shapes=[pltpu.VMEM((TOP_K, tm, D_MODEL), F32), pltpu.SemaphoreType.DMA(())],
        compiler_params=_params(),
        name="combine",
    )(pos, x1, rw, gfin, ys)


def _dispatch(experts, bm):
    t = experts.shape[0]
    a = t * TOP_K
    nb = -(-(a + N_EXPERTS * (bm - 1)) // bm)
    e_flat = experts.reshape(a)
    onehot = (e_flat[:, None] == jnp.arange(N_EXPERTS, dtype=jnp.int32)[None, :]).astype(jnp.int32)
    csum = jnp.cumsum(onehot, axis=0)
    rank = jnp.sum(csum * onehot, axis=1) - 1
    counts = csum[-1]
    padded = (counts + bm - 1) // bm * bm
    pend = jnp.cumsum(padded)
    pstart = pend - padded
    dest = pstart[e_flat] + rank
    n_used = (pend[-1] // bm).astype(jnp.int32)
    blk = jnp.arange(nb, dtype=jnp.int32)
    be = jnp.minimum(jnp.searchsorted(pend, blk * bm, side='right'), N_EXPERTS - 1).astype(jnp.int32)
    be = jnp.where(blk < n_used, be, be[jnp.maximum(n_used - 1, 0)])
    slot_tok = jnp.zeros((nb * bm,), jnp.int32).at[dest].set(jnp.arange(a, dtype=jnp.int32) // TOP_K)
    return dest.astype(jnp.int32), slot_tok.reshape(nb, 1, bm), be, n_used.reshape(1)


def _rope_table(pos):
    inv = 1.0 / (ROPE_THETA ** (jnp.arange(0, ROPE_DIM, 2, dtype=F32) / ROPE_DIM))
    ang = pos.astype(F32)[:, None] * inv[None, :]
    cos, sin = jnp.cos(ang), jnp.sin(ang)
    z = jnp.zeros((pos.shape[0], LANES - ROPE_DIM), F32)
    return jnp.concatenate([cos, cos, z, sin, sin, z], axis=1)


def _rotate_half_cols(w):
    return jnp.concatenate([-w[..., HALF_ROPE:], w[..., :HALF_ROPE]], axis=-1)


def kernel(x_prompt, x_sample, state_pool, cache_kv_latent, cache_k_rope, page_table, g_mix_norm, w_in, g_q_norm,
           g_kv_norm, w_uq, w_uk, w_uv, w_pool, pool_scale, w_o, g_ffn_norm, w_group_router, b_group_router,
           w_router, b_router, w_gate_e, w_up_e, w_down_e, g_final_norm):
    n_batch, seq, d = x_prompt.shape
    bd, sd, _ = x_sample.shape
    assert w_in.shape[0] == 1 and d == D_MODEL, "single-layer step only"
    n_pages = page_table.shape[1]
    past_len = n_pages * PAGE_SIZE
    tp, ts = n_batch * seq, bd * sd
    t = tp + ts
    assert tp % ts == 0 and seq % ROW_TILE == 0 and t % ROW_TILE == 0 and ts % ROW_TILE == 0

    w_in0 = w_in[0]
    o_kr = POOL_DIM + Q_LORA + KV_LORA
    wa = w_in0[:, :o_kr].astype(BF16)
    w_kr = w_in0[:, o_kr:o_kr + ROPE_DIM]
    zk = jnp.zeros((d, LANES - ROPE_DIM), F32)
    wkr = jnp.concatenate([w_kr, zk, _rotate_half_cols(w_kr), zk], axis=1).astype(BF16)
    wg = w_in0[:, o_kr + ROPE_DIM:].astype(BF16)
    nope, rope = w_uq[0][:, :, :NOPE_DIM], w_uq[0][:, :, NOPE_DIM:]
    zq = jnp.zeros((Q_LORA, N_HEADS, LANES - ROPE_DIM), F32)
    wq = jnp.concatenate([nope, rope, zq], axis=-1).reshape(Q_LORA, N_HEADS * QK_DIM).astype(BF16)
    wqp = jnp.concatenate([_rotate_half_cols(rope), zq], axis=-1).reshape(Q_LORA, N_HEADS * LANES).astype(BF16)
    wuk = w_uk[0].reshape(KV_LORA, N_HEADS * NOPE_DIM).astype(BF16)
    wuv = w_uv[0].reshape(KV_LORA, N_HEADS * V_DIM).astype(BF16)
    wukt = w_uk[0].transpose(1, 2, 0).astype(BF16)
    wuv_h = w_uv[0].transpose(1, 0, 2).astype(BF16)
    wr = jnp.concatenate([w_group_router[0], w_router[0],
                          jnp.zeros((d, LANES - N_GROUPS - N_EXPERTS), F32)], axis=1)
    wrhi = wr.astype(BF16)
    wrlo = (wr - wrhi.astype(F32)).astype(BF16)
    br = jnp.concatenate([b_group_router[0], b_router[0],
                          jnp.zeros((LANES - N_GROUPS - N_EXPERTS,), F32)]).reshape(1, LANES)
    row = lambda g: g.reshape(1, -1)

    pos = jnp.concatenate([jnp.tile(jnp.arange(seq, dtype=jnp.int32), n_batch),
                           past_len + jnp.tile(jnp.arange(sd, dtype=jnp.int32), bd)])
    cs = _rope_table(pos)

    x_p, x_s = x_prompt.reshape(tp, d), x_sample.reshape(ts, d)
    u, kv, kr, q, k, v = _inproj(x_p, x_s, row(g_mix_norm[0]), wa, wkr, row(g_q_norm[0]), row(g_kv_norm[0]),
                                 wq, wqp, wuk, wuv, cs)

    u_s = u[tp:].reshape(bd, sd, POOL_DIM)
    slab = HALO + 8
    u_full_s = jnp.concatenate([state_pool[0].astype(F32), u_s,
                                jnp.zeros((bd, slab - POOL_STATE_LEN - sd, POOL_DIM), F32)], axis=1)
    pooled_p = _pool(u, tp, seq)
    pooled_s = _pool(u_full_s.reshape(bd * slab, POOL_DIM), bd * slab, None)
    pooled_s = pooled_s.reshape(bd, slab, POOL_DIM)[:, POOL_STATE_LEN:POOL_STATE_LEN + sd].reshape(ts, POOL_DIM)

    o_prompt = _flash(q, k, v, n_batch, seq)
    qa, qr = _qabs(q, wukt, tp, ts)
    qa = qa.reshape(bd, sd * N_HEADS, KV_LORA)
    qr = qr.reshape(bd, sd * N_HEADS, LANES)
    pad_new = lambda z: jnp.pad(z.reshape(bd, sd, -1), ((0, 0), (0, NEW_ROWS_PAD - sd), (0, 0)))
    o_lat = _decode(page_table, qa, qr, pad_new(kv[tp:]), pad_new(kr[tp:]), cache_kv_latent,
                    jnp.swapaxes(cache_k_rope, 2, 3))
    o_sample = _ov(o_lat.reshape(ts, N_HEADS * KV_LORA), wuv_h)

    x1, hn, ridx, rw = _post(x_p, x_s, pooled_p, pooled_s, o_prompt, o_sample, row(g_mix_norm[0]), wg,
                             w_pool[0].astype(BF16), row(pool_scale[0]), w_o[0].astype(BF16), row(g_ffn_norm[0]),
                             wrhi, wrlo, br)

    dest, slot_tok, block_expert, n_used = _dispatch(ridx[:, :TOP_K], MOE_BLOCK)
    ys = _moe(block_expert, n_used, slot_tok, hn, w_gate_e, w_up_e, w_down_e)
    y_p, y_s = _combine(dest.reshape(t // COMBINE_TILE, 1, TOP_K * COMBINE_TILE), x1, rw, row(g_final_norm), ys, tp)

    u_p = u[:tp].reshape(n_batch, seq, POOL_DIM)
    pool_s = jnp.concatenate([state_pool[0].astype(F32), u_s], axis=1)[:, -POOL_STATE_LEN:]
    return (y_p.reshape(n_batch, seq, d), y_s.reshape(bd, sd, d),
            kv[:tp].reshape(1, n_batch, seq, KV_LORA), kr[:tp].reshape(1, n_batch, seq, ROPE_DIM),
            u_p[:, -POOL_STATE_LEN:][None],
            kv[tp:].reshape(1, bd, sd, KV_LORA), kr[tp:].reshape(1, bd, sd, ROPE_DIM), pool_s[None])
```

```python
import functools

import jax
import jax.numpy as jnp
from jax import lax
from jax.experimental import pallas as pl
from jax.experimental.pallas import tpu as pltpu

F32 = jnp.float32
BF16 = jnp.bfloat16

D_MODEL = 2048
POOL_WINDOWS = (2, 4, 8, 16)
POOL_GROUP_DIM = 256
POOL_DIM = 1024
POOL_OUT_GROUP_DIM = 512
POOL_STATE_LEN = 15
N_HEADS = 16
V_DIM = 128
Q_LORA = 512
KV_LORA = 512
NOPE_DIM = 128
ROPE_DIM = 64
HALF_ROPE = ROPE_DIM // 2
ROPE_THETA = 10000.0
SOFTMAX_SCALE = (NOPE_DIM + ROPE_DIM) ** -0.5
SCORE_SCALE_LOG2 = SOFTMAX_SCALE * 1.4426950408889634
PAGE_SIZE = 128
N_GROUPS = 8
EXPERTS_PER_GROUP = 8
N_EXPERTS = 64
TOP_K = 2
D_EXPERT = 1408
EPS = 1e-6

LANES = 128
QK_DIM = 2 * LANES
HALO = 16
ROW_TILE = 256
POST_TILE = 128
ATTN_TILE = 512
DECODE_STREAMS = 2
PAGES_PER_STREAM = 8
DECODE_SLOTS = 4
NEW_ROWS_PAD = 16
MOE_BLOCK = 384
FF_CHUNK = 256
EXPERT_OUT_CHUNK = 512
COMBINE_TILE = 128
WEIGHT_DMA_PRIORITY = 1
ROW_DMA_PRIORITY = 0
VMEM_LIMIT = 56 * 1024 * 1024


def _rms(xf, g):
    ms = jnp.mean(xf * xf, axis=-1, keepdims=True)
    return xf * lax.rsqrt(ms + EPS) * g


def _dot(a, b):
    return jnp.dot(a, b, preferred_element_type=F32)


def _dot_nt(a, b):
    return lax.dot_general(a, b, (((1,), (1,)), ((), ())), preferred_element_type=F32)


def _resident(shape):
    nd = len(shape)
    return pl.BlockSpec(shape, lambda *_: (0,) * nd, pipeline_mode=pl.Buffered(1))


def _params(**kw):
    return pltpu.CompilerParams(vmem_limit_bytes=VMEM_LIMIT, **kw)


def _split_rows(tm, width, n_first):
    return (pl.BlockSpec((tm, width), lambda i: (jnp.minimum(i, n_first - 1), 0)),
            pl.BlockSpec((tm, width), lambda i: (jnp.maximum(i - n_first, 0), 0)))


def _inproj_kernel(x_ref, xs_ref, gmix_ref, wa_ref, wkr_ref, gq_ref, gkv_ref, wq_ref, wqp_ref, wuk_ref, wuv_ref,
                   cs_ref, u_ref, kv_ref, kr_ref, q_ref, k_ref, v_ref, *, prompt_tiles):
    x = jnp.where(pl.program_id(0) < prompt_tiles, x_ref[...], xs_ref[...])
    xn = _rms(x, gmix_ref[...]).astype(BF16)
    u_ref[...] = _dot(xn, wa_ref[:, 0:POOL_DIM])
    ql = _rms(_dot(xn, wa_ref[:, POOL_DIM:POOL_DIM + Q_LORA]), gq_ref[...]).astype(BF16)
    kvf = _rms(_dot(xn, wa_ref[:, POOL_DIM + Q_LORA:]), gkv_ref[...])
    kv_ref[...] = kvf
    kvb = kvf.astype(BF16)
    cos = cs_ref[:, 0:LANES]
    sin = cs_ref[:, LANES:]
    hk = _dot(xn, wkr_ref[...])
    kr = hk[:, 0:LANES] * cos + hk[:, LANES:] * sin
    kr_ref[...] = kr[:, 0:ROPE_DIM]
    krb = kr.astype(BF16)
    for hp in range(N_HEADS // 2):
        qq = _dot(ql, wq_ref[:, hp * 2 * QK_DIM:(hp + 1) * 2 * QK_DIM])
        pp = _dot(ql, wqp_ref[:, hp * 2 * LANES:(hp + 1) * 2 * LANES])
        kn = _dot(kvb, wuk_ref[:, hp * 2 * NOPE_DIM:(hp + 1) * 2 * NOPE_DIM])
        vv = _dot(kvb, wuv_ref[:, hp * 2 * V_DIM:(hp + 1) * 2 * V_DIM])
        for s in range(2):
            h = 2 * hp + s
            qh = qq[:, s * QK_DIM:(s + 1) * QK_DIM]
            rope = qh[:, LANES:] * cos + pp[:, s * LANES:(s + 1) * LANES] * sin
            q_ref[h, :, 0:LANES] = qh[:, 0:LANES].astype(BF16)
            q_ref[h, :, LANES:] = rope.astype(BF16)
            k_ref[h, :, 0:LANES] = kn[:, s * NOPE_DIM:(s + 1) * NOPE_DIM].astype(BF16)
            k_ref[h, :, LANES:] = krb
            v_ref[h] = vv[:, s * V_DIM:(s + 1) * V_DIM].astype(BF16)


def _inproj(x_p, x_s, gmix, wa, wkr, gq, gkv, wq, wqp, wuk, wuv, cs):
    tm = ROW_TILE
    n_p = x_p.shape[0] // tm
    t = x_p.shape[0] + x_s.shape[0]
    row = lambda w: pl.BlockSpec((tm, w), lambda i: (i, 0))
    head = lambda w: pl.BlockSpec((N_HEADS, tm, w), lambda i: (0, i, 0))
    return pl.pallas_call(
        functools.partial(_inproj_kernel, prompt_tiles=n_p),
        grid=(t // tm,),
        in_specs=[*_split_rows(tm, D_MODEL, n_p), _resident(gmix.shape), _resident(wa.shape), _resident(wkr.shape),
                  _resident(gq.shape), _resident(gkv.shape), _resident(wq.shape), _resident(wqp.shape),
                  _resident(wuk.shape), _resident(wuv.shape), row(2 * LANES)],
        out_specs=[row(POOL_DIM), row(KV_LORA), row(ROPE_DIM), head(QK_DIM), head(QK_DIM), head(V_DIM)],
        out_shape=[jax.ShapeDtypeStruct((t, POOL_DIM), F32), jax.ShapeDtypeStruct((t, KV_LORA), F32),
                   jax.ShapeDtypeStruct((t, ROPE_DIM), F32), jax.ShapeDtypeStruct((N_HEADS, t, QK_DIM), BF16),
                   jax.ShapeDtypeStruct((N_HEADS, t, QK_DIM), BF16), jax.ShapeDtypeStruct((N_HEADS, t, V_DIM), BF16)],
        compiler_params=_params(),
        name="inproj",
    )(x_p, x_s, gmix, wa, wkr, gq, gkv, wq, wqp, wuk, wuv, cs)


def _pool_kernel(uprev_ref, u_ref, o_ref, buf_ref, *, tm, seq_len):
    i = pl.program_id(0)
    if seq_len is None:
        seq_start = i == 0
    else:
        pos0 = (i * tm) % seq_len
        seq_start = pos0 == 0
    buf_ref[0:HALO, :] = jnp.where(seq_start, 0.0, uprev_ref[...])
    buf_ref[HALO:HALO + tm, :] = u_ref[...]
    for gi, w in enumerate(POOL_WINDOWS):
        c0, c1 = gi * POOL_GROUP_DIM, (gi + 1) * POOL_GROUP_DIM
        u_new = buf_ref[HALO:HALO + tm, c0:c1]
        acc = u_new
        for j in range(1, w):
            acc = acc + buf_ref[HALO - j:HALO - j + tm, c0:c1]
        if seq_len is None:
            count = float(w)
        else:
            pos = pos0 + lax.broadcasted_iota(jnp.int32, (tm, 1), 0)
            count = jnp.minimum(w, pos + 1).astype(F32)
        o_ref[:, c0:c1] = acc / count - u_new


def _pool(u, n_rows, seq_len):
    tm = ROW_TILE
    per = tm // HALO
    return pl.pallas_call(
        functools.partial(_pool_kernel, tm=tm, seq_len=seq_len),
        grid=(n_rows // tm,),
        in_specs=[pl.BlockSpec((HALO, POOL_DIM), lambda i: (jnp.maximum(i * per - 1, 0), 0)),
                  pl.BlockSpec((tm, POOL_DIM), lambda i: (i, 0))],
        out_specs=pl.BlockSpec((tm, POOL_DIM), lambda i: (i, 0)),
        out_shape=jax.ShapeDtypeStruct((n_rows, POOL_DIM), F32),
        scratch_shapes=[pltpu.VMEM((HALO + tm, POOL_DIM), F32)],
        compiler_params=_params(),
        name="pool",
    )(u, u)


def _flash_kernel(q_ref, k_ref, v_ref, o_ref, *, tq):
    qi = pl.program_id(2)
    q = q_ref[...]

    def step(kb, carry, diagonal):
        m, l, acc = carry
        off = pl.multiple_of(kb * tq, tq)
        s = _dot_nt(q, k_ref[pl.ds(off, tq), :]) * SCORE_SCALE_LOG2
        if diagonal:
            r = lax.broadcasted_iota(jnp.int32, (tq, tq), 0)
            c = lax.broadcasted_iota(jnp.int32, (tq, tq), 1)
            s = jnp.where(c <= r, s, -jnp.inf)
        m_new = jnp.maximum(m, jnp.max(s, axis=-1, keepdims=True))
        alpha = jnp.exp2(m - m_new)
        p = jnp.exp2(s - m_new)
        l = alpha * l + jnp.sum(p, axis=-1, keepdims=True)
        acc = alpha * acc + _dot(p.astype(BF16), v_ref[pl.ds(off, tq), :])
        return m_new, l, acc

    init = (jnp.full((tq, 1), -jnp.inf, F32), jnp.zeros((tq, 1), F32), jnp.zeros((tq, V_DIM), F32))
    carry = lax.fori_loop(0, qi, lambda kb, c: step(kb, c, False), init)
    _, l, acc = step(qi, carry, True)
    o_ref[...] = acc / l


def _flash(q, k, v, n_batch, seq):
    tq = ATTN_TILE
    nq = seq // tq
    return pl.pallas_call(
        functools.partial(_flash_kernel, tq=tq),
        grid=(n_batch, N_HEADS, nq),
        in_specs=[pl.BlockSpec((None, tq, QK_DIM), lambda b, h, i: (h, b * nq + i, 0)),
                  pl.BlockSpec((None, seq, QK_DIM), lambda b, h, i: (h, b, 0)),
                  pl.BlockSpec((None, seq, V_DIM), lambda b, h, i: (h, b, 0))],
        out_specs=pl.BlockSpec((tq, V_DIM), lambda b, h, i: (b * nq + i, h)),
        out_shape=jax.ShapeDtypeStruct((n_batch * seq, N_HEADS * V_DIM), F32),
        compiler_params=_params(),
        name="flash",
    )(q, k, v)


def _qabs_kernel(q_ref, wukt_ref, qa_ref, qr_ref):
    qa_ref[...] = _dot(q_ref[:, 0:NOPE_DIM], wukt_ref[...]).astype(BF16)
    qr_ref[...] = q_ref[:, LANES:]


def _qabs(q, wukt, row0, n_rows):
    blk = row0 // n_rows
    return pl.pallas_call(
        _qabs_kernel,
        grid=(N_HEADS,),
        in_specs=[pl.BlockSpec((None, n_rows, QK_DIM), lambda h: (h, blk, 0)),
                  pl.BlockSpec((None, NOPE_DIM, KV_LORA), lambda h: (h, 0, 0))],
        out_specs=[pl.BlockSpec((n_rows, KV_LORA), lambda h: (0, h)),
                   pl.BlockSpec((n_rows, LANES), lambda h: (0, h))],
        out_shape=[jax.ShapeDtypeStruct((n_rows, N_HEADS * KV_LORA), BF16),
                   jax.ShapeDtypeStruct((n_rows, N_HEADS * LANES), BF16)],
        compiler_params=_params(),
        name="qabs",
    )(q, wukt)


def _decode_kernel(pt_ref, qa_ref, qr_ref, kvn_ref, krn_ref, kv_hbm, krt_hbm, o_ref,
                   kv_buf, krt_buf, m_ref, l_ref, acc_ref, kv_sem, kr_sem, *, n_chunks, heads):
    b = pl.program_id(0)
    n_seq = pl.num_programs(0)
    n_slots, cp = kv_buf.shape[0], kv_buf.shape[1]
    ahead = n_slots - 1
    per_stream = cp // DECODE_STREAMS

    def page_copies(g, slot, i):
        page = pt_ref[g // n_chunks, (g % n_chunks) * cp + i]
        return (pltpu.make_async_copy(kv_hbm.at[0, page], kv_buf.at[slot, i], kv_sem.at[slot]),
                pltpu.make_async_copy(krt_hbm.at[0, page], krt_buf.at[slot, i], kr_sem.at[slot]))

    def start_chunk(g, slot):
        for i in range(cp):
            kv_copy, kr_copy = page_copies(g, slot, i)
            kv_copy.start(priority=i % 2)
            kr_copy.start(priority=(i + 1) % 2)

    def wait_chunk(g, slot):
        for i in range(cp):
            for copy in page_copies(g, slot, i):
                copy.wait()

    @pl.when(b == 0)
    def _():
        for c in range(ahead):
            start_chunk(c, c)

    m_ref[...] = jnp.full(m_ref.shape, -jnp.inf, F32)
    l_ref[...] = jnp.zeros(l_ref.shape, F32)
    acc_ref[...] = jnp.zeros(acc_ref.shape, F32)
    qa = qa_ref[...]
    qr = qr_ref[:, 0:ROPE_DIM]

    def consume(slot):
        for t in range(DECODE_STREAMS):
            p0 = t * per_stream
            kvb = kv_buf[slot, p0:p0 + per_stream].reshape(per_stream * PAGE_SIZE, KV_LORA).astype(BF16)
            krt = jnp.concatenate([krt_buf[slot, p0 + i] for i in range(per_stream)], axis=1).astype(BF16)
            s = (_dot_nt(qa, kvb) + _dot(qr, krt)) * SCORE_SCALE_LOG2
            m = m_ref[t]
            m_new = jnp.maximum(m, jnp.max(s, axis=-1, keepdims=True))
            alpha = jnp.exp2(m - m_new)
            p = jnp.exp2(s - m_new)
            l_ref[t] = alpha * l_ref[t] + jnp.sum(p, axis=-1, keepdims=True)
            acc_ref[t] = alpha * acc_ref[t] + _dot(p.astype(BF16), kvb)
            m_ref[t] = m_new

    def ring_turn(cc, carry):
        for slot in range(n_slots):
            g = b * n_chunks + cc * n_slots + slot
            wait_chunk(g, slot)

            @pl.when(g + ahead < n_seq * n_chunks)
            def _():
                start_chunk(g + ahead, (slot + ahead) % n_slots)

            consume(slot)
        return carry

    lax.fori_loop(0, n_chunks // n_slots, ring_turn, 0)

    kvn = kvn_ref[...].astype(BF16)
    s_new = (_dot_nt(qa, kvn) + _dot_nt(qr, krn_ref[...].astype(BF16))) * SCORE_SCALE_LOG2
    tok = lax.broadcasted_iota(jnp.int32, s_new.shape, 0) // heads
    col = lax.broadcasted_iota(jnp.int32, s_new.shape, 1)
    s_new = jnp.where(col <= tok, s_new, -jnp.inf)
    m = jnp.max(s_new, axis=-1, keepdims=True)
    for t in range(DECODE_STREAMS):
        m = jnp.maximum(m, m_ref[t])
    p_new = jnp.exp2(s_new - m)
    l = jnp.sum(p_new, axis=-1, keepdims=True)
    acc = _dot(p_new.astype(BF16), kvn)
    for t in range(DECODE_STREAMS):
        w = jnp.exp2(m_ref[t] - m)
        l = l + w * l_ref[t]
        acc = acc + w * acc_ref[t]
    o_ref[...] = acc / l


def _decode(page_table, qa, qr, kvn, krn, cache_kv, cache_krt):
    bd, n_pages = page_table.shape
    rows = qa.shape[1]
    cp = DECODE_STREAMS * PAGES_PER_STREAM
    n_chunks = n_pages // cp
    ns = DECODE_SLOTS
    assert n_pages % cp == 0 and n_chunks % ns == 0
    per_seq = lambda r, w: pl.BlockSpec((None, r, w), lambda b, pt: (b, 0, 0))
    grid_spec = pltpu.PrefetchScalarGridSpec(
        num_scalar_prefetch=1,
        grid=(bd,),
        in_specs=[per_seq(rows, KV_LORA), per_seq(rows, LANES), per_seq(NEW_ROWS_PAD, KV_LORA),
                  per_seq(NEW_ROWS_PAD, ROPE_DIM), pl.BlockSpec(memory_space=pl.ANY),
                  pl.BlockSpec(memory_space=pl.ANY)],
        out_specs=per_seq(rows, KV_LORA),
        scratch_shapes=[pltpu.VMEM((ns, cp, PAGE_SIZE, KV_LORA), F32), pltpu.VMEM((ns, cp, ROPE_DIM, PAGE_SIZE), F32),
                        pltpu.VMEM((DECODE_STREAMS, rows, 1), F32), pltpu.VMEM((DECODE_STREAMS, rows, 1), F32),
                        pltpu.VMEM((DECODE_STREAMS, rows, KV_LORA), F32),
                        pltpu.SemaphoreType.DMA((ns,)), pltpu.SemaphoreType.DMA((ns,))],
    )
    return pl.pallas_call(
        functools.partial(_decode_kernel, n_chunks=n_chunks, heads=N_HEADS),
        grid_spec=grid_spec,
        out_shape=jax.ShapeDtypeStruct((bd, rows, KV_LORA), F32),
        compiler_params=_params(),
        name="decode",
    )(page_table, qa, qr, kvn, krn, cache_kv, cache_krt)


def _ov_kernel(olat_ref, wuv_ref, o_ref):
    o_ref[...] = _dot(olat_ref[...].astype(BF16), wuv_ref[...])


def _ov(olat2d, wuv_h):
    n_rows = olat2d.shape[0]
    return pl.pallas_call(
        _ov_kernel,
        grid=(N_HEADS,),
        in_specs=[pl.BlockSpec((n_rows, KV_LORA), lambda h: (0, h)),
                  pl.BlockSpec((None, KV_LORA, V_DIM), lambda h: (h, 0, 0))],
        out_specs=pl.BlockSpec((n_rows, V_DIM), lambda h: (0, h)),
        out_shape=jax.ShapeDtypeStruct((n_rows, N_HEADS * V_DIM), F32),
        compiler_params=_params(),
        name="ov",
    )(olat2d, wuv_h)


def _post_kernel(xp_ref, xs_ref, pp_ref, ps_ref, op_ref, os_ref, gmix_ref, wg_ref, wpool_ref, pscale_ref, wo_ref,
                 gffn_ref, wrhi_ref, wrlo_ref, br_ref, x1_ref, hn_ref, ridx_ref, rw_ref, merged_ref, *,
                 prompt_tiles):
    is_prompt = pl.program_id(0) < prompt_tiles
    x = jnp.where(is_prompt, xp_ref[...], xs_ref[...])
    xn = _rms(x, gmix_ref[...]).astype(BF16)
    w = POOL_OUT_GROUP_DIM
    for g in range(len(POOL_WINDOWS)):
        cols = slice(g * w, (g + 1) * w)
        pcols = slice(g * POOL_GROUP_DIM, (g + 1) * POOL_GROUP_DIM)
        pooled = jnp.where(is_prompt, pp_ref[:, pcols], ps_ref[:, pcols]).astype(BF16)
        y_pool = _dot(pooled, wpool_ref[g]) * pscale_ref[:, cols]
        gate_pool = jax.nn.sigmoid(_dot(xn, wg_ref[:, cols]))
        gate_mla = jax.nn.sigmoid(_dot(xn, wg_ref[:, D_MODEL + g * w:D_MODEL + (g + 1) * w]))
        y_mla = jnp.where(is_prompt, op_ref[:, cols], os_ref[:, cols])
        merged_ref[:, cols] = (gate_pool * y_pool + gate_mla * y_mla).astype(BF16)
    x1 = x + _dot(merged_ref[...], wo_ref[...])
    x1_ref[...] = x1
    hn = _rms(x1, gffn_ref[...])
    hn_ref[...] = hn

    hi = hn.astype(BF16)
    lo = (hn - hi.astype(F32)).astype(BF16)
    logits = _dot(hi, wrhi_ref[...]) + (_dot(hi, wrlo_ref[...]) + _dot(lo, wrhi_ref[...])) + br_ref[...]

    lane = lax.broadcasted_iota(jnp.int32, logits.shape, 1)
    lane_f = lane.astype(F32)
    none = float(LANES)
    gl = jnp.where(lane < N_GROUPS, logits, -jnp.inf)
    g_max = jnp.max(gl, axis=-1, keepdims=True)
    g_sel = jnp.min(jnp.where(gl == g_max, lane_f, none), axis=-1, keepdims=True)
    g_w = 1.0 / jnp.sum(jnp.exp(gl - g_max), axis=-1, keepdims=True)
    first = N_GROUPS + EXPERTS_PER_GROUP * g_sel
    el = jnp.where((lane_f >= first) & (lane_f < first + EXPERTS_PER_GROUP), logits, -jnp.inf)
    t1 = jnp.max(el, axis=-1, keepdims=True)
    i1 = jnp.min(jnp.where(el == t1, lane_f, none), axis=-1, keepdims=True)
    el2 = jnp.where(lane_f == i1, -jnp.inf, el)
    t2 = jnp.max(el2, axis=-1, keepdims=True)
    i2 = jnp.min(jnp.where(el2 == t2, lane_f, none), axis=-1, keepdims=True)
    e2 = jnp.exp(t2 - t1)
    w1 = g_w / (1.0 + e2)
    w2 = g_w * e2 / (1.0 + e2)
    ridx = jnp.where(lane == 0, i1 - N_GROUPS, jnp.where(lane == 1, i2 - N_GROUPS, 0.0))
    ridx_ref[...] = ridx.astype(jnp.int32)
    rw_ref[...] = jnp.where(lane == 0, w1, jnp.where(lane == 1, w2, 0.0))


def _post(x_p, x_s, pooled_p, pooled_s, o_prompt, o_sample, gmix, wg, wpool, pscale, wo, gffn, wrhi, wrlo, br):
    t = x_p.shape[0] + x_s.shape[0]
    tm = POST_TILE
    n_p = x_p.shape[0] // tm
    row = lambda w: pl.BlockSpec((tm, w), lambda i: (i, 0))
    return pl.pallas_call(
        functools.partial(_post_kernel, prompt_tiles=n_p),
        grid=(t // tm,),
        in_specs=[*_split_rows(tm, D_MODEL, n_p), *_split_rows(tm, POOL_DIM, n_p), *_split_rows(tm, D_MODEL, n_p),
                  _resident(gmix.shape), _resident(wg.shape),
                  _resident(wpool.shape), _resident(pscale.shape), _resident(wo.shape), _resident(gffn.shape),
                  _resident(wrhi.shape), _resident(wrlo.shape), _resident(br.shape)],
        out_specs=[row(D_MODEL), row(D_MODEL), row(LANES), row(LANES)],
        out_shape=[jax.ShapeDtypeStruct((t, D_MODEL), F32), jax.ShapeDtypeStruct((t, D_MODEL), F32),
                   jax.ShapeDtypeStruct((t, LANES), jnp.int32), jax.ShapeDtypeStruct((t, LANES), F32)],
        scratch_shapes=[pltpu.VMEM((tm, D_MODEL), BF16)],
        compiler_params=_params(),
        name="post",
    )(x_p, x_s, pooled_p, pooled_s, o_prompt, o_sample, gmix, wg, wpool, pscale, wo, gffn, wrhi, wrlo, br)


def _row_copy(src_hbm, src_row, dst_ref, dst_row, sem):
    return pltpu.make_async_copy(src_hbm.at[pl.ds(src_row, 1), :], dst_ref.at[pl.ds(dst_row, 1), :], sem)


def _moe_kernel(be_ref, nbu_ref, tok_next_ref, tok_first_ref, hn_hbm, wg_hbm, wu_hbm, wd_hbm, o_ref,
                x_buf, act_ref, gu_buf, d_buf, x_sem, g_sem, u_sem, d_sem):
    b = pl.program_id(0)
    n_used = nbu_ref[0]
    bm = x_buf.shape[1]
    ff_stages = [(c0, min(FF_CHUNK, D_EXPERT - c0)) for c0 in range(0, D_EXPERT, FF_CHUNK)]
    n_gu = len(ff_stages)
    n_dn = D_MODEL // EXPERT_OUT_CHUNK
    assert n_gu % 2 == 0 and n_dn % 2 == 0 and bm % n_gu == 0

    def gate_up_copies(e, j, slot):
        c0, width = ff_stages[j]
        cols = pl.ds(c0, width)
        return (pltpu.make_async_copy(wg_hbm.at[0, e, :, cols], gu_buf.at[slot, 0, :, pl.ds(0, width)], g_sem.at[slot]),
                pltpu.make_async_copy(wu_hbm.at[0, e, :, cols], gu_buf.at[slot, 1, :, pl.ds(0, width)], u_sem.at[slot]))

    def down_copy(e, c, slot):
        cols = pl.ds(c * EXPERT_OUT_CHUNK, EXPERT_OUT_CHUNK)
        return pltpu.make_async_copy(wd_hbm.at[0, e, :, cols], d_buf.at[slot], d_sem.at[slot])

    def start_gate_up(e, j, slot):
        for copy in gate_up_copies(e, j, slot):
            copy.start(priority=WEIGHT_DMA_PRIORITY)

    def start_rows(tok_ref, slot, r0, r1):
        for r in range(r0, r1):
            _row_copy(hn_hbm, tok_ref[0, 0, r], x_buf.at[slot], r, x_sem.at[slot]).start(priority=ROW_DMA_PRIORITY)

    def wait_rows(slot):
        def body(r, carry):
            _row_copy(hn_hbm, 0, x_buf.at[slot], r, x_sem.at[slot]).wait()
            return carry
        lax.fori_loop(0, bm, body, 0)

    @pl.when(b == 0)
    def _():
        start_gate_up(be_ref[0], 0, 0)
        start_gate_up(be_ref[0], 1, 1)

        def body(r, carry):
            _row_copy(hn_hbm, tok_first_ref[0, 0, r], x_buf.at[0], r, x_sem.at[0]).start(priority=ROW_DMA_PRIORITY)
            return carry
        lax.fori_loop(0, bm, body, 0)

    @pl.when(b < n_used)
    def _():
        e = be_ref[b]
        e_next = be_ref[b + 1]
        x_slot = b % 2
        down_copy(e, 0, 0).start(priority=WEIGHT_DMA_PRIORITY)
        down_copy(e, 1, 1).start(priority=WEIGHT_DMA_PRIORITY)
        wait_rows(x_slot)
        xb = x_buf[x_slot].astype(BF16)
        rows_per_stage = bm // n_gu

        def fetch_gate_up(j):
            slot = j % 2
            width = ff_stages[j][1]
            for copy in gate_up_copies(e, j, slot):
                copy.wait()
            w = jnp.concatenate([gu_buf[slot, 0, :, 0:width].astype(BF16),
                                 gu_buf[slot, 1, :, 0:width].astype(BF16)], axis=1)
            if j + 2 < n_gu:
                start_gate_up(e, j + 2, slot)
            else:
                start_gate_up(e_next, j + 2 - n_gu, slot)
            return w

        def fetch_down(c):
            slot = c % 2
            down_copy(e, c, slot).wait()
            w = d_buf[slot].astype(BF16)
            if c + 2 < n_dn:
                down_copy(e, c + 2, slot).start(priority=WEIGHT_DMA_PRIORITY)
            return w

        w = fetch_gate_up(0)
        for j in range(n_gu):
            w_next = fetch_gate_up(j + 1) if j + 1 < n_gu else fetch_down(0)
            start_rows(tok_next_ref, 1 - x_slot, j * rows_per_stage, (j + 1) * rows_per_stage)
            c0, width = ff_stages[j]
            h = _dot(xb, w)
            act = jax.nn.silu(h[:, 0:width]) * h[:, width:]
            act_ref[:, c0:c0 + width] = act.astype(BF16)
            w = w_next
        act_all = act_ref[...]
        for c in range(n_dn):
            w_next = fetch_down(c + 1) if c + 1 < n_dn else None
            o_ref[:, c * EXPERT_OUT_CHUNK:(c + 1) * EXPERT_OUT_CHUNK] = _dot(act_all, w)
            w = w_next

    @pl.when(b == n_used)
    def _():
        wait_rows(b % 2)
        for slot in range(2):
            for copy in gate_up_copies(be_ref[b], slot, slot):
                copy.wait()

    @pl.when(b >= n_used)
    def _():
        o_ref[...] = jnp.zeros(o_ref.shape, o_ref.dtype)


def _moe(block_expert, n_used, slot_tok, hn, w_gate, w_up, w_down):
    nb, _, bm = slot_tok.shape
    tok_spec = lambda index_map: pl.BlockSpec((1, 1, bm), index_map, memory_space=pltpu.SMEM)
    grid_spec = pltpu.PrefetchScalarGridSpec(
        num_scalar_prefetch=2,
        grid=(nb,),
        in_specs=[tok_spec(lambda b, be, nbu: (jnp.minimum(b + 1, nb - 1), 0, 0)),
                  tok_spec(lambda b, be, nbu: (0, 0, 0)),
                  pl.BlockSpec(memory_space=pl.ANY), pl.BlockSpec(memory_space=pl.ANY),
                  pl.BlockSpec(memory_space=pl.ANY), pl.BlockSpec(memory_space=pl.ANY)],
        out_specs=pl.BlockSpec((bm, D_MODEL), lambda b, be, nbu: (b, 0)),
        scratch_shapes=[pltpu.VMEM((2, bm, D_MODEL), F32), pltpu.VMEM((bm, D_EXPERT), BF16),
                        pltpu.VMEM((2, 2, D_MODEL, FF_CHUNK), F32), pltpu.VMEM((2, D_EXPERT, EXPERT_OUT_CHUNK), F32),
                        pltpu.SemaphoreType.DMA((2,)), pltpu.SemaphoreType.DMA((2,)),
                        pltpu.SemaphoreType.DMA((2,)), pltpu.SemaphoreType.DMA((2,))],
    )
    return pl.pallas_call(
        _moe_kernel,
        grid_spec=grid_spec,
        out_shape=jax.ShapeDtypeStruct((nb * bm, D_MODEL), F32),
        compiler_params=_params(),
        name="moe",
    )(block_expert, n_used, slot_tok, slot_tok, hn, w_gate, w_up, w_down)


def _combine_kernel(pos_first_ref, pos_next_ref, x1_ref, rw_ref, gfin_ref, ys_hbm, yp_ref, ys_ref, buf_ref, sem, *,
                    prompt_tiles):
    i = pl.program_id(0)
    last = pl.num_programs(0) - 1
    n = x1_ref.shape[0]
    slot = i % 2

    def copy(pos_ref, dst_slot, r, k):
        return _row_copy(ys_hbm, pos_ref[0, 0, TOP_K * r + k], buf_ref.at[dst_slot, k], r, sem.at[dst_slot])

    def wait_slot(dst_slot):
        def body(r, carry):
            for k in range(TOP_K):
                copy(pos_next_ref, dst_slot, r, k).wait()
            return carry
        lax.fori_loop(0, n, body, 0)

    @pl.when(i == 0)
    def _():
        def body(r, carry):
            for k in range(TOP_K):
                copy(pos_first_ref, 0, r, k).start(priority=k)
            return carry
        lax.fori_loop(0, n, body, 0)

    wait_slot(slot)
    for r in range(n):
        for k in range(TOP_K):
            copy(pos_next_ref, 1 - slot, r, k).start(priority=k)
    rw = rw_ref[...]
    moe = rw[:, 0:1] * buf_ref[slot, 0] + rw[:, 1:2] * buf_ref[slot, 1]
    y = _rms(x1_ref[...] + moe, gfin_ref[...])

    @pl.when(i == last)
    def _():
        wait_slot(1 - slot)

    is_prompt = i < prompt_tiles

    @pl.when(is_prompt)
    def _():
        yp_ref[...] = y

    @pl.when(jnp.logical_not(is_prompt))
    def _():
        ys_ref[...] = y


def _combine(pos, x1, rw, gfin, ys, n_prompt_rows):
    t = x1.shape[0]
    tm = COMBINE_TILE
    n_p = n_prompt_rows // tm
    n_tiles = t // tm
    pos_spec = lambda index_map: pl.BlockSpec((1, 1, TOP_K * tm), index_map, memory_space=pltpu.SMEM)
    return pl.pallas_call(
        functools.partial(_combine_kernel, prompt_tiles=n_p),
        grid=(n_tiles,),
        in_specs=[pos_spec(lambda i: (0, 0, 0)),
                  pos_spec(lambda i: (jnp.minimum(i + 1, n_tiles - 1), 0, 0)),
                  pl.BlockSpec((tm, D_MODEL), lambda i: (i, 0)),
                  pl.BlockSpec((tm, LANES), lambda i: (i, 0)),
                  pl.BlockSpec(gfin.shape, lambda i: (0, 0)),
                  pl.BlockSpec(memory_space=pl.ANY)],
        out_specs=list(_split_rows(tm, D_MODEL, n_p)),
        out_shape=[jax.ShapeDtypeStruct((n_prompt_rows, D_MODEL), F32),
                   jax.ShapeDtypeStruct((t - n_prompt_rows, D_MODEL), F32)],
        scratch_shapes=[pltpu.VMEM((2, TOP_K, tm, D_MODEL), F32), pltpu.SemaphoreType.DMA((2,))],
        compiler_params=_params(),
        name="combine",
    )(pos, pos, x1, rw, gfin, ys)


def _dispatch(experts, bm):
    t = experts.shape[0]
    a = t * TOP_K
    nb = -(-(a + N_EXPERTS * (bm - 1)) // bm) + 1
    e_flat = experts.reshape(a)
    onehot = (e_flat[:, None] == jnp.arange(N_EXPERTS, dtype=jnp.int32)[None, :]).astype(jnp.int32)
    csum = jnp.cumsum(onehot, axis=0)
    rank = jnp.sum(csum * onehot, axis=1) - 1
    counts = csum[-1]
    padded = (counts + bm - 1) // bm * bm
    pend = jnp.cumsum(padded)
    pstart = pend - padded
    dest = pstart[e_flat] + rank
    n_used = (pend[-1] // bm).astype(jnp.int32)
    blk = jnp.arange(nb, dtype=jnp.int32)
    be = jnp.minimum(jnp.searchsorted(pend, blk * bm, side='right'), N_EXPERTS - 1).astype(jnp.int32)
    be = jnp.where(blk < n_used, be, be[jnp.maximum(n_used - 1, 0)])
    slot_tok = jnp.zeros((nb * bm,), jnp.int32).at[dest].set(jnp.arange(a, dtype=jnp.int32) // TOP_K)
    return dest.astype(jnp.int32), slot_tok.reshape(nb, 1, bm), be, n_used.reshape(1)


def _rope_table(pos):
    inv = 1.0 / (ROPE_THETA ** (jnp.arange(0, ROPE_DIM, 2, dtype=F32) / ROPE_DIM))
    ang = pos.astype(F32)[:, None] * inv[None, :]
    cos, sin = jnp.cos(ang), jnp.sin(ang)
    z = jnp.zeros((pos.shape[0], LANES - ROPE_DIM), F32)
    return jnp.concatenate([cos, cos, z, sin, sin, z], axis=1)


def _rotate_half_cols(w):
    return jnp.concatenate([-w[..., HALF_ROPE:], w[..., :HALF_ROPE]], axis=-1)


def kernel(x_prompt, x_sample, state_pool, cache_kv_latent, cache_k_rope, page_table, g_mix_norm, w_in, g_q_norm,
           g_kv_norm, w_uq, w_uk, w_uv, w_pool, pool_scale, w_o, g_ffn_norm, w_group_router, b_group_router,
           w_router, b_router, w_gate_e, w_up_e, w_down_e, g_final_norm):
    n_batch, seq, d = x_prompt.shape
    bd, sd, _ = x_sample.shape
    assert w_in.shape[0] == 1 and d == D_MODEL, "single-layer step only"
    n_pages = page_table.shape[1]
    past_len = n_pages * PAGE_SIZE
    tp, ts = n_batch * seq, bd * sd
    t = tp + ts
    assert tp % ts == 0 and seq % ROW_TILE == 0 and t % ROW_TILE == 0 and ts % ROW_TILE == 0

    w_in0 = w_in[0]
    o_kr = POOL_DIM + Q_LORA + KV_LORA
    wa = w_in0[:, :o_kr].astype(BF16)
    w_kr = w_in0[:, o_kr:o_kr + ROPE_DIM]
    zk = jnp.zeros((d, LANES - ROPE_DIM), F32)
    wkr = jnp.concatenate([w_kr, zk, _rotate_half_cols(w_kr), zk], axis=1).astype(BF16)
    wg = w_in0[:, o_kr + ROPE_DIM:].astype(BF16)
    nope, rope = w_uq[0][:, :, :NOPE_DIM], w_uq[0][:, :, NOPE_DIM:]
    zq = jnp.zeros((Q_LORA, N_HEADS, LANES - ROPE_DIM), F32)
    wq = jnp.concatenate([nope, rope, zq], axis=-1).reshape(Q_LORA, N_HEADS * QK_DIM).astype(BF16)
    wqp = jnp.concatenate([_rotate_half_cols(rope), zq], axis=-1).reshape(Q_LORA, N_HEADS * LANES).astype(BF16)
    wuk = w_uk[0].reshape(KV_LORA, N_HEADS * NOPE_DIM).astype(BF16)
    wuv = w_uv[0].reshape(KV_LORA, N_HEADS * V_DIM).astype(BF16)
    wukt = w_uk[0].transpose(1, 2, 0).astype(BF16)
    wuv_h = w_uv[0].transpose(1, 0, 2).astype(BF16)
    wr = jnp.concatenate([w_group_router[0], w_router[0],
                          jnp.zeros((d, LANES - N_GROUPS - N_EXPERTS), F32)], axis=1)
    wrhi = wr.astype(BF16)
    wrlo = (wr - wrhi.astype(F32)).astype(BF16)
    br = jnp.concatenate([b_group_router[0], b_router[0],
                          jnp.zeros((LANES - N_GROUPS - N_EXPERTS,), F32)]).reshape(1, LANES)
    row = lambda g: g.reshape(1, -1)

    pos = jnp.concatenate([jnp.tile(jnp.arange(seq, dtype=jnp.int32), n_batch),
                           past_len + jnp.tile(jnp.arange(sd, dtype=jnp.int32), bd)])
    cs = _rope_table(pos)

    x_p, x_s = x_prompt.reshape(tp, d), x_sample.reshape(ts, d)
    u, kv, kr, q, k, v = _inproj(x_p, x_s, row(g_mix_norm[0]), wa, wkr, row(g_q_norm[0]), row(g_kv_norm[0]),
                                 wq, wqp, wuk, wuv, cs)

    u_s = u[tp:].reshape(bd, sd, POOL_DIM)
    slab = HALO + 8
    u_full_s = jnp.concatenate([state_pool[0].astype(F32), u_s,
                                jnp.zeros((bd, slab - POOL_STATE_LEN - sd, POOL_DIM), F32)], axis=1)
    pooled_p = _pool(u, tp, seq)
    pooled_s = _pool(u_full_s.reshape(bd * slab, POOL_DIM), bd * slab, None)
    pooled_s = pooled_s.reshape(bd, slab, POOL_DIM)[:, POOL_STATE_LEN:POOL_STATE_LEN + sd].reshape(ts, POOL_DIM)

    o_prompt = _flash(q, k, v, n_batch, seq)
    qa, qr = _qabs(q, wukt, tp, ts)
    qa = qa.reshape(bd, sd * N_HEADS, KV_LORA)
    qr = qr.reshape(bd, sd * N_HEADS, LANES)
    pad_new = lambda z: jnp.pad(z.reshape(bd, sd, -1), ((0, 0), (0, NEW_ROWS_PAD - sd), (0, 0)))
    o_lat = _decode(page_table, qa, qr, pad_new(kv[tp:]), pad_new(kr[tp:]), cache_kv_latent,
                    jnp.swapaxes(cache_k_rope, 2, 3))
    o_sample = _ov(o_lat.reshape(ts, N_HEADS * KV_LORA), wuv_h)

    x1, hn, ridx, rw = _post(x_p, x_s, pooled_p, pooled_s, o_prompt, o_sample, row(g_mix_norm[0]), wg,
                             w_pool[0].astype(BF16), row(pool_scale[0]), w_o[0].astype(BF16), row(g_ffn_norm[0]),
                             wrhi, wrlo, br)

    dest, slot_tok, block_expert, n_used = _dispatch(ridx[:, :TOP_K], MOE_BLOCK)
    ys = _moe(block_expert, n_used, slot_tok, hn, w_gate_e, w_up_e, w_down_e)
    y_p, y_s = _combine(dest.reshape(t // COMBINE_TILE, 1, TOP_K * COMBINE_TILE), x1, rw, row(g_final_norm), ys, tp)

    u_p = u[:tp].reshape(n_batch, seq, POOL_DIM)
    pool_s = jnp.concatenate([state_pool[0].astype(F32), u_s], axis=1)[:, -POOL_STATE_LEN:]
    return (y_p.reshape(n_batch, seq, d), y_s.reshape(bd, sd, d),
            kv[:tp].reshape(1, n_batch, seq, KV_LORA), kr[:tp].reshape(1, n_batch, seq, ROPE_DIM),
            u_p[:, -POOL_STATE_LEN:][None],
            kv[tp:].reshape(1, bd, sd, KV_LORA), kr[tp:].reshape(1, bd, sd, ROPE_DIM), pool_s[None])
```

```python
import functools

import jax
import jax.numpy as jnp
from jax import lax
from jax.experimental import pallas as pl
from jax.experimental.pallas import tpu as pltpu

F32 = jnp.float32
BF16 = jnp.bfloat16

D_MODEL = 2048
POOL_WINDOWS = (2, 4, 8, 16)
POOL_GROUP_DIM = 256
POOL_DIM = 1024
POOL_OUT_GROUP_DIM = 512
POOL_STATE_LEN = 15
N_HEADS = 16
V_DIM = 128
Q_LORA = 512
KV_LORA = 512
NOPE_DIM = 128
ROPE_DIM = 64
HALF_ROPE = ROPE_DIM // 2
ROPE_THETA = 10000.0
SOFTMAX_SCALE = (NOPE_DIM + ROPE_DIM) ** -0.5
SCORE_SCALE_LOG2 = SOFTMAX_SCALE * 1.4426950408889634
PAGE_SIZE = 128
N_GROUPS = 8
EXPERTS_PER_GROUP = 8
N_EXPERTS = 64
TOP_K = 2
D_EXPERT = 1408
EPS = 1e-6

LANES = 128
QK_DIM = 2 * LANES
HALO = 16
ROW_TILE = 256
POST_TILE = 128
ATTN_Q_TILE = 1024
ATTN_K_TILE = 512
DECODE_STREAMS = 2
PAGES_PER_STREAM = 8
DECODE_SLOTS = 4
NEW_ROWS_PAD = 16
MOE_BLOCK = 384
GATE_UP_ROWS = 256
DOWN_ROWS = 512
COMBINE_TILE = 128
WEIGHT_DMA_PRIORITY = 1
ROW_DMA_PRIORITY = 0
VMEM_LIMIT = 56 * 1024 * 1024


def _rms(xf, g):
    ms = jnp.mean(xf * xf, axis=-1, keepdims=True)
    return xf * lax.rsqrt(ms + EPS) * g


def _dot(a, b):
    return jnp.dot(a, b, preferred_element_type=F32)


def _dot_nt(a, b):
    return lax.dot_general(a, b, (((1,), (1,)), ((), ())), preferred_element_type=F32)


def _resident(shape):
    nd = len(shape)
    return pl.BlockSpec(shape, lambda *_: (0,) * nd, pipeline_mode=pl.Buffered(1))


def _params(**kw):
    return pltpu.CompilerParams(vmem_limit_bytes=VMEM_LIMIT, **kw)


def _split_rows(tm, width, n_first):
    return (pl.BlockSpec((tm, width), lambda i: (jnp.minimum(i, n_first - 1), 0)),
            pl.BlockSpec((tm, width), lambda i: (jnp.maximum(i - n_first, 0), 0)))


def _inproj_kernel(x_ref, xs_ref, gmix_ref, wa_ref, wkr_ref, gq_ref, gkv_ref, wq_ref, wqp_ref, wuk_ref, wuv_ref,
                   cs_ref, u_ref, kv_ref, kr_ref, q_ref, k_ref, v_ref, *, prompt_tiles):
    x = jnp.where(pl.program_id(0) < prompt_tiles, x_ref[...], xs_ref[...])
    xn = _rms(x, gmix_ref[...]).astype(BF16)
    u_ref[...] = _dot(xn, wa_ref[:, 0:POOL_DIM])
    ql = _rms(_dot(xn, wa_ref[:, POOL_DIM:POOL_DIM + Q_LORA]), gq_ref[...]).astype(BF16)
    kvf = _rms(_dot(xn, wa_ref[:, POOL_DIM + Q_LORA:]), gkv_ref[...])
    kv_ref[...] = kvf
    kvb = kvf.astype(BF16)
    cos = cs_ref[:, 0:LANES]
    sin = cs_ref[:, LANES:]
    hk = _dot(xn, wkr_ref[...])
    kr = hk[:, 0:LANES] * cos + hk[:, LANES:] * sin
    kr_ref[...] = kr[:, 0:ROPE_DIM]
    krb = kr.astype(BF16)
    for hp in range(N_HEADS // 2):
        qq = _dot(ql, wq_ref[:, hp * 2 * QK_DIM:(hp + 1) * 2 * QK_DIM])
        pp = _dot(ql, wqp_ref[:, hp * 2 * LANES:(hp + 1) * 2 * LANES])
        kn = _dot(kvb, wuk_ref[:, hp * 2 * NOPE_DIM:(hp + 1) * 2 * NOPE_DIM])
        vv = _dot(kvb, wuv_ref[:, hp * 2 * V_DIM:(hp + 1) * 2 * V_DIM])
        for s in range(2):
            h = 2 * hp + s
            qh = qq[:, s * QK_DIM:(s + 1) * QK_DIM]
            rope = qh[:, LANES:] * cos + pp[:, s * LANES:(s + 1) * LANES] * sin
            q_ref[h, :, 0:LANES] = qh[:, 0:LANES].astype(BF16)
            q_ref[h, :, LANES:] = rope.astype(BF16)
            k_ref[h, :, 0:LANES] = kn[:, s * NOPE_DIM:(s + 1) * NOPE_DIM].astype(BF16)
            k_ref[h, :, LANES:] = krb
            v_ref[h] = vv[:, s * V_DIM:(s + 1) * V_DIM].astype(BF16)


def _inproj(x_p, x_s, gmix, wa, wkr, gq, gkv, wq, wqp, wuk, wuv, cs):
    tm = ROW_TILE
    n_p = x_p.shape[0] // tm
    t = x_p.shape[0] + x_s.shape[0]
    row = lambda w: pl.BlockSpec((tm, w), lambda i: (i, 0))
    head = lambda w: pl.BlockSpec((N_HEADS, tm, w), lambda i: (0, i, 0))
    return pl.pallas_call(
        functools.partial(_inproj_kernel, prompt_tiles=n_p),
        grid=(t // tm,),
        in_specs=[*_split_rows(tm, D_MODEL, n_p), _resident(gmix.shape), _resident(wa.shape), _resident(wkr.shape),
                  _resident(gq.shape), _resident(gkv.shape), _resident(wq.shape), _resident(wqp.shape),
                  _resident(wuk.shape), _resident(wuv.shape), row(2 * LANES)],
        out_specs=[row(POOL_DIM), row(KV_LORA), row(ROPE_DIM), head(QK_DIM), head(QK_DIM), head(V_DIM)],
        out_shape=[jax.ShapeDtypeStruct((t, POOL_DIM), F32), jax.ShapeDtypeStruct((t, KV_LORA), F32),
                   jax.ShapeDtypeStruct((t, ROPE_DIM), F32), jax.ShapeDtypeStruct((N_HEADS, t, QK_DIM), BF16),
                   jax.ShapeDtypeStruct((N_HEADS, t, QK_DIM), BF16), jax.ShapeDtypeStruct((N_HEADS, t, V_DIM), BF16)],
        compiler_params=_params(),
        name="inproj",
    )(x_p, x_s, gmix, wa, wkr, gq, gkv, wq, wqp, wuk, wuv, cs)


def _pool_kernel(uprev_ref, u_ref, o_ref, buf_ref, *, tm, seq_len):
    i = pl.program_id(0)
    if seq_len is None:
        seq_start = i == 0
    else:
        pos0 = (i * tm) % seq_len
        seq_start = pos0 == 0
    buf_ref[0:HALO, :] = jnp.where(seq_start, 0.0, uprev_ref[...])
    buf_ref[HALO:HALO + tm, :] = u_ref[...]
    for gi, w in enumerate(POOL_WINDOWS):
        c0, c1 = gi * POOL_GROUP_DIM, (gi + 1) * POOL_GROUP_DIM
        u_new = buf_ref[HALO:HALO + tm, c0:c1]
        acc = u_new
        for j in range(1, w):
            acc = acc + buf_ref[HALO - j:HALO - j + tm, c0:c1]
        if seq_len is None:
            count = float(w)
        else:
            pos = pos0 + lax.broadcasted_iota(jnp.int32, (tm, 1), 0)
            count = jnp.minimum(w, pos + 1).astype(F32)
        o_ref[:, c0:c1] = acc / count - u_new


def _pool(u, n_rows, seq_len):
    tm = ROW_TILE
    per = tm // HALO
    return pl.pallas_call(
        functools.partial(_pool_kernel, tm=tm, seq_len=seq_len),
        grid=(n_rows // tm,),
        in_specs=[pl.BlockSpec((HALO, POOL_DIM), lambda i: (jnp.maximum(i * per - 1, 0), 0)),
                  pl.BlockSpec((tm, POOL_DIM), lambda i: (i, 0))],
        out_specs=pl.BlockSpec((tm, POOL_DIM), lambda i: (i, 0)),
        out_shape=jax.ShapeDtypeStruct((n_rows, POOL_DIM), F32),
        scratch_shapes=[pltpu.VMEM((HALO + tm, POOL_DIM), F32)],
        compiler_params=_params(),
        name="pool",
    )(u, u)


def _flash_kernel(q_ref, k_ref, v_ref, o_ref, *, tq, tk):
    qi = pl.program_id(2)
    q = q_ref[...]

    def step(kb, carry, diagonal):
        m, l, acc = carry
        off = pl.multiple_of(kb * tk, tk)
        s = _dot_nt(q, k_ref[pl.ds(off, tk), :]) * SCORE_SCALE_LOG2
        if diagonal:
            r = qi * tq + lax.broadcasted_iota(jnp.int32, (tq, tk), 0)
            c = off + lax.broadcasted_iota(jnp.int32, (tq, tk), 1)
            s = jnp.where(c <= r, s, -jnp.inf)
        m_new = jnp.maximum(m, jnp.max(s, axis=-1, keepdims=True))
        alpha = jnp.exp2(m - m_new)
        p = jnp.exp2(s - m_new)
        l = alpha * l + jnp.sum(p, axis=-1, keepdims=True)
        acc = alpha * acc + _dot(p.astype(BF16), v_ref[pl.ds(off, tk), :])
        return m_new, l, acc

    n_full = qi * (tq // tk)
    init = (jnp.full((tq, 1), -jnp.inf, F32), jnp.zeros((tq, 1), F32), jnp.zeros((tq, V_DIM), F32))
    carry = lax.fori_loop(0, n_full, lambda kb, c: step(kb, c, False), init)
    for d in range(tq // tk):
        carry = step(n_full + d, carry, True)
    _, l, acc = carry
    o_ref[...] = acc / l


def _flash(q, k, v, n_batch, seq):
    tq, tk = ATTN_Q_TILE, ATTN_K_TILE
    nq = seq // tq
    return pl.pallas_call(
        functools.partial(_flash_kernel, tq=tq, tk=tk),
        grid=(n_batch, N_HEADS, nq),
        in_specs=[pl.BlockSpec((None, tq, QK_DIM), lambda b, h, i: (h, b * nq + i, 0)),
                  pl.BlockSpec((None, seq, QK_DIM), lambda b, h, i: (h, b, 0)),
                  pl.BlockSpec((None, seq, V_DIM), lambda b, h, i: (h, b, 0))],
        out_specs=pl.BlockSpec((tq, V_DIM), lambda b, h, i: (b * nq + i, h)),
        out_shape=jax.ShapeDtypeStruct((n_batch * seq, N_HEADS * V_DIM), F32),
        compiler_params=_params(),
        name="flash",
    )(q, k, v)


def _qabs_kernel(q_ref, wukt_ref, qa_ref, qr_ref):
    qa_ref[...] = _dot(q_ref[:, 0:NOPE_DIM], wukt_ref[...]).astype(BF16)
    qr_ref[...] = q_ref[:, LANES:]


def _qabs(q, wukt, row0, n_rows):
    blk = row0 // n_rows
    return pl.pallas_call(
        _qabs_kernel,
        grid=(N_HEADS,),
        in_specs=[pl.BlockSpec((None, n_rows, QK_DIM), lambda h: (h, blk, 0)),
                  pl.BlockSpec((None, NOPE_DIM, KV_LORA), lambda h: (h, 0, 0))],
        out_specs=[pl.BlockSpec((n_rows, KV_LORA), lambda h: (0, h)),
                   pl.BlockSpec((n_rows, LANES), lambda h: (0, h))],
        out_shape=[jax.ShapeDtypeStruct((n_rows, N_HEADS * KV_LORA), BF16),
                   jax.ShapeDtypeStruct((n_rows, N_HEADS * LANES), BF16)],
        compiler_params=_params(),
        name="qabs",
    )(q, wukt)


def _decode_kernel(pt_ref, qa_ref, qr_ref, kvn_ref, krn_ref, kv_hbm, krt_hbm, o_ref,
                   kv_buf, krt_buf, m_ref, l_ref, acc_ref, kv_sem, kr_sem, *, n_chunks, heads):
    b = pl.program_id(0)
    n_seq = pl.num_programs(0)
    n_slots, cp = kv_buf.shape[0], kv_buf.shape[1]
    ahead = n_slots - 1
    per_stream = cp // DECODE_STREAMS

    def page_copies(g, slot, i):
        page = pt_ref[g // n_chunks, (g % n_chunks) * cp + i]
        return (pltpu.make_async_copy(kv_hbm.at[0, page], kv_buf.at[slot, i], kv_sem.at[slot]),
                pltpu.make_async_copy(krt_hbm.at[0, page], krt_buf.at[slot, i], kr_sem.at[slot]))

    def start_chunk(g, slot):
        for i in range(cp):
            kv_copy, kr_copy = page_copies(g, slot, i)
            kv_copy.start(priority=i % 2)
            kr_copy.start(priority=(i + 1) % 2)

    def wait_chunk(g, slot):
        for i in range(cp):
            for copy in page_copies(g, slot, i):
                copy.wait()

    @pl.when(b == 0)
    def _():
        for c in range(ahead):
            start_chunk(c, c)

    m_ref[...] = jnp.full(m_ref.shape, -jnp.inf, F32)
    l_ref[...] = jnp.zeros(l_ref.shape, F32)
    acc_ref[...] = jnp.zeros(acc_ref.shape, F32)
    qa = qa_ref[...]
    qr = qr_ref[:, 0:ROPE_DIM]

    def consume(slot):
        for t in range(DECODE_STREAMS):
            p0 = t * per_stream
            kvb = kv_buf[slot, p0:p0 + per_stream].reshape(per_stream * PAGE_SIZE, KV_LORA).astype(BF16)
            krt = jnp.concatenate([krt_buf[slot, p0 + i] for i in range(per_stream)], axis=1).astype(BF16)
            s = (_dot_nt(qa, kvb) + _dot(qr, krt)) * SCORE_SCALE_LOG2
            m = m_ref[t]
            m_new = jnp.maximum(m, jnp.max(s, axis=-1, keepdims=True))
            alpha = jnp.exp2(m - m_new)
            p = jnp.exp2(s - m_new)
            l_ref[t] = alpha * l_ref[t] + jnp.sum(p, axis=-1, keepdims=True)
            acc_ref[t] = alpha * acc_ref[t] + _dot(p.astype(BF16), kvb)
            m_ref[t] = m_new

    def ring_turn(cc, carry):
        for slot in range(n_slots):
            g = b * n_chunks + cc * n_slots + slot
            wait_chunk(g, slot)

            @pl.when(g + ahead < n_seq * n_chunks)
            def _():
                start_chunk(g + ahead, (slot + ahead) % n_slots)

            consume(slot)
        return carry

    lax.fori_loop(0, n_chunks // n_slots, ring_turn, 0)

    kvn = kvn_ref[...].astype(BF16)
    s_new = (_dot_nt(qa, kvn) + _dot_nt(qr, krn_ref[...].astype(BF16))) * SCORE_SCALE_LOG2
    tok = lax.broadcasted_iota(jnp.int32, s_new.shape, 0) // heads
    col = lax.broadcasted_iota(jnp.int32, s_new.shape, 1)
    s_new = jnp.where(col <= tok, s_new, -jnp.inf)
    m = jnp.max(s_new, axis=-1, keepdims=True)
    for t in range(DECODE_STREAMS):
        m = jnp.maximum(m, m_ref[t])
    p_new = jnp.exp2(s_new - m)
    l = jnp.sum(p_new, axis=-1, keepdims=True)
    acc = _dot(p_new.astype(BF16), kvn)
    for t in range(DECODE_STREAMS):
        w = jnp.exp2(m_ref[t] - m)
        l = l + w * l_ref[t]
        acc = acc + w * acc_ref[t]
    o_ref[...] = acc / l


def _decode(page_table, qa, qr, kvn, krn, cache_kv, cache_krt):
    bd, n_pages = page_table.shape
    rows = qa.shape[1]
    cp = DECODE_STREAMS * PAGES_PER_STREAM
    n_chunks = n_pages // cp
    ns = DECODE_SLOTS
    assert n_pages % cp == 0 and n_chunks % ns == 0
    per_seq = lambda r, w: pl.BlockSpec((None, r, w), lambda b, pt: (b, 0, 0))
    grid_spec = pltpu.PrefetchScalarGridSpec(
        num_scalar_prefetch=1,
        grid=(bd,),
        in_specs=[per_seq(rows, KV_LORA), per_seq(rows, LANES), per_seq(NEW_ROWS_PAD, KV_LORA),
                  per_seq(NEW_ROWS_PAD, ROPE_DIM), pl.BlockSpec(memory_space=pl.ANY),
                  pl.BlockSpec(memory_space=pl.ANY)],
        out_specs=per_seq(rows, KV_LORA),
        scratch_shapes=[pltpu.VMEM((ns, cp, PAGE_SIZE, KV_LORA), F32), pltpu.VMEM((ns, cp, ROPE_DIM, PAGE_SIZE), F32),
                        pltpu.VMEM((DECODE_STREAMS, rows, 1), F32), pltpu.VMEM((DECODE_STREAMS, rows, 1), F32),
                        pltpu.VMEM((DECODE_STREAMS, rows, KV_LORA), F32),
                        pltpu.SemaphoreType.DMA((ns,)), pltpu.SemaphoreType.DMA((ns,))],
    )
    return pl.pallas_call(
        functools.partial(_decode_kernel, n_chunks=n_chunks, heads=N_HEADS),
        grid_spec=grid_spec,
        out_shape=jax.ShapeDtypeStruct((bd, rows, KV_LORA), F32),
        compiler_params=_params(),
        name="decode",
    )(page_table, qa, qr, kvn, krn, cache_kv, cache_krt)


def _ov_kernel(olat_ref, wuv_ref, o_ref):
    o_ref[...] = _dot(olat_ref[...].astype(BF16), wuv_ref[...])


def _ov(olat2d, wuv_h):
    n_rows = olat2d.shape[0]
    return pl.pallas_call(
        _ov_kernel,
        grid=(N_HEADS,),
        in_specs=[pl.BlockSpec((n_rows, KV_LORA), lambda h: (0, h)),
                  pl.BlockSpec((None, KV_LORA, V_DIM), lambda h: (h, 0, 0))],
        out_specs=pl.BlockSpec((n_rows, V_DIM), lambda h: (0, h)),
        out_shape=jax.ShapeDtypeStruct((n_rows, N_HEADS * V_DIM), F32),
        compiler_params=_params(),
        name="ov",
    )(olat2d, wuv_h)


def _post_kernel(xp_ref, xs_ref, pp_ref, ps_ref, op_ref, os_ref, gmix_ref, wg_ref, wpool_ref, pscale_ref, wo_ref,
                 gffn_ref, wrhi_ref, wrlo_ref, br_ref, x1_ref, hn_ref, ridx_ref, rw_ref, merged_ref, *,
                 prompt_tiles):
    is_prompt = pl.program_id(0) < prompt_tiles
    x = jnp.where(is_prompt, xp_ref[...], xs_ref[...])
    xn = _rms(x, gmix_ref[...]).astype(BF16)
    w = POOL_OUT_GROUP_DIM
    for g in range(len(POOL_WINDOWS)):
        cols = slice(g * w, (g + 1) * w)
        pcols = slice(g * POOL_GROUP_DIM, (g + 1) * POOL_GROUP_DIM)
        pooled = jnp.where(is_prompt, pp_ref[:, pcols], ps_ref[:, pcols]).astype(BF16)
        y_pool = _dot(pooled, wpool_ref[g]) * pscale_ref[:, cols]
        gate_pool = jax.nn.sigmoid(_dot(xn, wg_ref[:, cols]))
        gate_mla = jax.nn.sigmoid(_dot(xn, wg_ref[:, D_MODEL + g * w:D_MODEL + (g + 1) * w]))
        y_mla = jnp.where(is_prompt, op_ref[:, cols], os_ref[:, cols])
        merged_ref[:, cols] = (gate_pool * y_pool + gate_mla * y_mla).astype(BF16)
    x1 = x + _dot(merged_ref[...], wo_ref[...])
    x1_ref[...] = x1
    hn = _rms(x1, gffn_ref[...])
    hn_ref[...] = hn

    hi = hn.astype(BF16)
    lo = (hn - hi.astype(F32)).astype(BF16)
    logits = _dot(hi, wrhi_ref[...]) + (_dot(hi, wrlo_ref[...]) + _dot(lo, wrhi_ref[...])) + br_ref[...]

    lane = lax.broadcasted_iota(jnp.int32, logits.shape, 1)
    lane_f = lane.astype(F32)
    none = float(LANES)
    gl = jnp.where(lane < N_GROUPS, logits, -jnp.inf)
    g_max = jnp.max(gl, axis=-1, keepdims=True)
    g_sel = jnp.min(jnp.where(gl == g_max, lane_f, none), axis=-1, keepdims=True)
    g_w = 1.0 / jnp.sum(jnp.exp(gl - g_max), axis=-1, keepdims=True)
    first = N_GROUPS + EXPERTS_PER_GROUP * g_sel
    el = jnp.where((lane_f >= first) & (lane_f < first + EXPERTS_PER_GROUP), logits, -jnp.inf)
    t1 = jnp.max(el, axis=-1, keepdims=True)
    i1 = jnp.min(jnp.where(el == t1, lane_f, none), axis=-1, keepdims=True)
    el2 = jnp.where(lane_f == i1, -jnp.inf, el)
    t2 = jnp.max(el2, axis=-1, keepdims=True)
    i2 = jnp.min(jnp.where(el2 == t2, lane_f, none), axis=-1, keepdims=True)
    e2 = jnp.exp(t2 - t1)
    w1 = g_w / (1.0 + e2)
    w2 = g_w * e2 / (1.0 + e2)
    ridx = jnp.where(lane == 0, i1 - N_GROUPS, jnp.where(lane == 1, i2 - N_GROUPS, 0.0))
    ridx_ref[...] = ridx.astype(jnp.int32)
    rw_ref[...] = jnp.where(lane == 0, w1, jnp.where(lane == 1, w2, 0.0))


def _post(x_p, x_s, pooled_p, pooled_s, o_prompt, o_sample, gmix, wg, wpool, pscale, wo, gffn, wrhi, wrlo, br):
    t = x_p.shape[0] + x_s.shape[0]
    tm = POST_TILE
    n_p = x_p.shape[0] // tm
    row = lambda w: pl.BlockSpec((tm, w), lambda i: (i, 0))
    return pl.pallas_call(
        functools.partial(_post_kernel, prompt_tiles=n_p),
        grid=(t // tm,),
        in_specs=[*_split_rows(tm, D_MODEL, n_p), *_split_rows(tm, POOL_DIM, n_p), *_split_rows(tm, D_MODEL, n_p),
                  _resident(gmix.shape), _resident(wg.shape),
                  _resident(wpool.shape), _resident(pscale.shape), _resident(wo.shape), _resident(gffn.shape),
                  _resident(wrhi.shape), _resident(wrlo.shape), _resident(br.shape)],
        out_specs=[row(D_MODEL), row(D_MODEL), row(LANES), row(LANES)],
        out_shape=[jax.ShapeDtypeStruct((t, D_MODEL), F32), jax.ShapeDtypeStruct((t, D_MODEL), F32),
                   jax.ShapeDtypeStruct((t, LANES), jnp.int32), jax.ShapeDtypeStruct((t, LANES), F32)],
        scratch_shapes=[pltpu.VMEM((tm, D_MODEL), BF16)],
        compiler_params=_params(),
        name="post",
    )(x_p, x_s, pooled_p, pooled_s, o_prompt, o_sample, gmix, wg, wpool, pscale, wo, gffn, wrhi, wrlo, br)


def _row_copy(src_hbm, src_row, dst_ref, dst_row, sem):
    return pltpu.make_async_copy(src_hbm.at[pl.ds(src_row, 1), :], dst_ref.at[pl.ds(dst_row, 1), :], sem)


def _moe_kernel(be_ref, nbu_ref, tok_next_ref, tok_first_ref, hn_hbm, wg_hbm, wu_hbm, wd_hbm, o_ref,
                x_buf, act_ref, gu_buf, d_buf, x_sem, g_sem, u_sem, d_sem):
    b = pl.program_id(0)
    n_used = nbu_ref[0]
    bm = x_buf.shape[1]
    kg = gu_buf.shape[2]
    n_gu = D_MODEL // kg
    kd = d_buf.shape[1]
    down_stages = [(r0, min(kd, D_EXPERT - r0)) for r0 in range(0, D_EXPERT, kd)]
    n_dn = len(down_stages)
    assert n_gu % 2 == 0 and bm % n_gu == 0

    def gate_up_copies(e, j, slot):
        rows = pl.ds(j * kg, kg)
        return (pltpu.make_async_copy(wg_hbm.at[0, e, rows, :], gu_buf.at[slot, 0], g_sem.at[slot]),
                pltpu.make_async_copy(wu_hbm.at[0, e, rows, :], gu_buf.at[slot, 1], u_sem.at[slot]))

    def down_copy(e, c, slot):
        r0, n = down_stages[c]
        return pltpu.make_async_copy(wd_hbm.at[0, e, pl.ds(r0, n), :], d_buf.at[slot, pl.ds(0, n), :], d_sem.at[slot])

    def start_gate_up(e, j, slot):
        for copy in gate_up_copies(e, j, slot):
            copy.start(priority=WEIGHT_DMA_PRIORITY)

    def start_rows(tok_ref, slot, r0, r1):
        for r in range(r0, r1):
            _row_copy(hn_hbm, tok_ref[0, 0, r], x_buf.at[slot], r, x_sem.at[slot]).start(priority=ROW_DMA_PRIORITY)

    def wait_rows(slot):
        def body(r, carry):
            _row_copy(hn_hbm, 0, x_buf.at[slot], r, x_sem.at[slot]).wait()
            return carry
        lax.fori_loop(0, bm, body, 0)

    @pl.when(b == 0)
    def _():
        start_gate_up(be_ref[0], 0, 0)
        start_gate_up(be_ref[0], 1, 1)

        def body(r, carry):
            _row_copy(hn_hbm, tok_first_ref[0, 0, r], x_buf.at[0], r, x_sem.at[0]).start(priority=ROW_DMA_PRIORITY)
            return carry
        lax.fori_loop(0, bm, body, 0)

    @pl.when(b < n_used)
    def _():
        e = be_ref[b]
        e_next = be_ref[b + 1]
        x_slot = b % 2
        down_copy(e, 0, 0).start(priority=WEIGHT_DMA_PRIORITY)
        down_copy(e, 1, 1).start(priority=WEIGHT_DMA_PRIORITY)
        wait_rows(x_slot)
        xb = x_buf[x_slot].astype(BF16)
        rows_per_stage = bm // n_gu

        def fetch_gate_up(j):
            slot = j % 2
            for copy in gate_up_copies(e, j, slot):
                copy.wait()
            w = (gu_buf[slot, 0].astype(BF16), gu_buf[slot, 1].astype(BF16))
            if j + 2 < n_gu:
                start_gate_up(e, j + 2, slot)
            else:
                start_gate_up(e_next, j + 2 - n_gu, slot)
            return w

        def fetch_down(c):
            slot = c % 2
            down_copy(e, c, slot).wait()
            w = d_buf[slot, 0:down_stages[c][1]].astype(BF16)
            if c + 2 < n_dn:
                down_copy(e, c + 2, slot).start(priority=WEIGHT_DMA_PRIORITY)
            return w

        w = fetch_gate_up(0)
        h_gate = h_up = None
        for j in range(n_gu):
            w_next = fetch_gate_up(j + 1) if j + 1 < n_gu else fetch_down(0)
            start_rows(tok_next_ref, 1 - x_slot, j * rows_per_stage, (j + 1) * rows_per_stage)
            xk = xb[:, j * kg:(j + 1) * kg]
            d_gate, d_up = _dot(xk, w[0]), _dot(xk, w[1])
            h_gate = d_gate if j == 0 else h_gate + d_gate
            h_up = d_up if j == 0 else h_up + d_up
            w = w_next
        act_ref[...] = (jax.nn.silu(h_gate) * h_up).astype(BF16)
        out = None
        for c in range(n_dn):
            w_next = fetch_down(c + 1) if c + 1 < n_dn else None
            r0, n = down_stages[c]
            part = _dot(act_ref[:, r0:r0 + n], w)
            out = part if c == 0 else out + part
            w = w_next
        o_ref[...] = out

    @pl.when(b == n_used)
    def _():
        wait_rows(b % 2)
        for slot in range(2):
            for copy in gate_up_copies(be_ref[b], slot, slot):
                copy.wait()

    @pl.when(b >= n_used)
    def _():
        o_ref[...] = jnp.zeros(o_ref.shape, o_ref.dtype)


def _moe(block_expert, n_used, slot_tok, hn, w_gate, w_up, w_down):
    nb, _, bm = slot_tok.shape
    tok_spec = lambda index_map: pl.BlockSpec((1, 1, bm), index_map, memory_space=pltpu.SMEM)
    grid_spec = pltpu.PrefetchScalarGridSpec(
        num_scalar_prefetch=2,
        grid=(nb,),
        in_specs=[tok_spec(lambda b, be, nbu: (jnp.minimum(b + 1, nb - 1), 0, 0)),
                  tok_spec(lambda b, be, nbu: (0, 0, 0)),
                  pl.BlockSpec(memory_space=pl.ANY), pl.BlockSpec(memory_space=pl.ANY),
                  pl.BlockSpec(memory_space=pl.ANY), pl.BlockSpec(memory_space=pl.ANY)],
        out_specs=pl.BlockSpec((bm, D_MODEL), lambda b, be, nbu: (b, 0)),
        scratch_shapes=[pltpu.VMEM((2, bm, D_MODEL), F32), pltpu.VMEM((bm, D_EXPERT), BF16),
                        pltpu.VMEM((2, 2, GATE_UP_ROWS, D_EXPERT), F32), pltpu.VMEM((2, DOWN_ROWS, D_MODEL), F32),
                        pltpu.SemaphoreType.DMA((2,)), pltpu.SemaphoreType.DMA((2,)),
                        pltpu.SemaphoreType.DMA((2,)), pltpu.SemaphoreType.DMA((2,))],
    )
    return pl.pallas_call(
        _moe_kernel,
        grid_spec=grid_spec,
        out_shape=jax.ShapeDtypeStruct((nb * bm, D_MODEL), F32),
        compiler_params=_params(),
        name="moe",
    )(block_expert, n_used, slot_tok, slot_tok, hn, w_gate, w_up, w_down)


def _combine_kernel(pos_first_ref, pos_next_ref, x1_ref, rw_ref, gfin_ref, ys_hbm, yp_ref, ys_ref, buf_ref, sem, *,
                    prompt_tiles):
    i = pl.program_id(0)
    last = pl.num_programs(0) - 1
    n = x1_ref.shape[0]
    slot = i % 2

    def copy(pos_ref, dst_slot, r, k):
        return _row_copy(ys_hbm, pos_ref[0, 0, TOP_K * r + k], buf_ref.at[dst_slot, k], r, sem.at[dst_slot])

    def wait_slot(dst_slot):
        def body(r, carry):
            for k in range(TOP_K):
                copy(pos_next_ref, dst_slot, r, k).wait()
            return carry
        lax.fori_loop(0, n, body, 0)

    @pl.when(i == 0)
    def _():
        def body(r, carry):
            for k in range(TOP_K):
                copy(pos_first_ref, 0, r, k).start(priority=k)
            return carry
        lax.fori_loop(0, n, body, 0)

    wait_slot(slot)
    for r in range(n):
        for k in range(TOP_K):
            copy(pos_next_ref, 1 - slot, r, k).start(priority=k)
    rw = rw_ref[...]
    moe = rw[:, 0:1] * buf_ref[slot, 0] + rw[:, 1:2] * buf_ref[slot, 1]
    y = _rms(x1_ref[...] + moe, gfin_ref[...])

    @pl.when(i == last)
    def _():
        wait_slot(1 - slot)

    is_prompt = i < prompt_tiles

    @pl.when(is_prompt)
    def _():
        yp_ref[...] = y

    @pl.when(jnp.logical_not(is_prompt))
    def _():
        ys_ref[...] = y


def _combine(pos, x1, rw, gfin, ys, n_prompt_rows):
    t = x1.shape[0]
    tm = COMBINE_TILE
    n_p = n_prompt_rows // tm
    n_tiles = t // tm
    pos_spec = lambda index_map: pl.BlockSpec((1, 1, TOP_K * tm), index_map, memory_space=pltpu.SMEM)
    return pl.pallas_call(
        functools.partial(_combine_kernel, prompt_tiles=n_p),
        grid=(n_tiles,),
        in_specs=[pos_spec(lambda i: (0, 0, 0)),
                  pos_spec(lambda i: (jnp.minimum(i + 1, n_tiles - 1), 0, 0)),
                  pl.BlockSpec((tm, D_MODEL), lambda i: (i, 0)),
                  pl.BlockSpec((tm, LANES), lambda i: (i, 0)),
                  pl.BlockSpec(gfin.shape, lambda i: (0, 0)),
                  pl.BlockSpec(memory_space=pl.ANY)],
        out_specs=list(_split_rows(tm, D_MODEL, n_p)),
        out_shape=[jax.ShapeDtypeStruct((n_prompt_rows, D_MODEL), F32),
                   jax.ShapeDtypeStruct((t - n_prompt_rows, D_MODEL), F32)],
        scratch_shapes=[pltpu.VMEM((2, TOP_K, tm, D_MODEL), F32), pltpu.SemaphoreType.DMA((2,))],
        compiler_params=_params(),
        name="combine",
    )(pos, pos, x1, rw, gfin, ys)


def _dispatch(experts, bm):
    t = experts.shape[0]
    a = t * TOP_K
    nb = -(-(a + N_EXPERTS * (bm - 1)) // bm) + 1
    e_flat = experts.reshape(a)
    onehot = (e_flat[:, None] == jnp.arange(N_EXPERTS, dtype=jnp.int32)[None, :]).astype(jnp.int32)
    csum = jnp.cumsum(onehot, axis=0)
    rank = jnp.sum(csum * onehot, axis=1) - 1
    counts = csum[-1]
    padded = (counts + bm - 1) // bm * bm
    pend = jnp.cumsum(padded)
    pstart = pend - padded
    dest = pstart[e_flat] + rank
    n_used = (pend[-1] // bm).astype(jnp.int32)
    blk = jnp.arange(nb, dtype=jnp.int32)
    be = jnp.minimum(jnp.searchsorted(pend, blk * bm, side='right'), N_EXPERTS - 1).astype(jnp.int32)
    be = jnp.where(blk < n_used, be, be[jnp.maximum(n_used - 1, 0)])
    slot_tok = jnp.zeros((nb * bm,), jnp.int32).at[dest].set(jnp.arange(a, dtype=jnp.int32) // TOP_K)
    return dest.astype(jnp.int32), slot_tok.reshape(nb, 1, bm), be, n_used.reshape(1)


def _rope_table(pos):
    inv = 1.0 / (ROPE_THETA ** (jnp.arange(0, ROPE_DIM, 2, dtype=F32) / ROPE_DIM))
    ang = pos.astype(F32)[:, None] * inv[None, :]
    cos, sin = jnp.cos(ang), jnp.sin(ang)
    z = jnp.zeros((pos.shape[0], LANES - ROPE_DIM), F32)
    return jnp.concatenate([cos, cos, z, sin, sin, z], axis=1)


def _rotate_half_cols(w):
    return jnp.concatenate([-w[..., HALF_ROPE:], w[..., :HALF_ROPE]], axis=-1)


def kernel(x_prompt, x_sample, state_pool, cache_kv_latent, cache_k_rope, page_table, g_mix_norm, w_in, g_q_norm,
           g_kv_norm, w_uq, w_uk, w_uv, w_pool, pool_scale, w_o, g_ffn_norm, w_group_router, b_group_router,
           w_router, b_router, w_gate_e, w_up_e, w_down_e, g_final_norm):
    n_batch, seq, d = x_prompt.shape
    bd, sd, _ = x_sample.shape
    assert w_in.shape[0] == 1 and d == D_MODEL, "single-layer step only"
    n_pages = page_table.shape[1]
    past_len = n_pages * PAGE_SIZE
    tp, ts = n_batch * seq, bd * sd
    t = tp + ts
    assert tp % ts == 0 and seq % ROW_TILE == 0 and t % ROW_TILE == 0 and ts % ROW_TILE == 0

    w_in0 = w_in[0]
    o_kr = POOL_DIM + Q_LORA + KV_LORA
    wa = w_in0[:, :o_kr].astype(BF16)
    w_kr = w_in0[:, o_kr:o_kr + ROPE_DIM]
    zk = jnp.zeros((d, LANES - ROPE_DIM), F32)
    wkr = jnp.concatenate([w_kr, zk, _rotate_half_cols(w_kr), zk], axis=1).astype(BF16)
    wg = w_in0[:, o_kr + ROPE_DIM:].astype(BF16)
    nope, rope = w_uq[0][:, :, :NOPE_DIM], w_uq[0][:, :, NOPE_DIM:]
    zq = jnp.zeros((Q_LORA, N_HEADS, LANES - ROPE_DIM), F32)
    wq = jnp.concatenate([nope, rope, zq], axis=-1).reshape(Q_LORA, N_HEADS * QK_DIM).astype(BF16)
    wqp = jnp.concatenate([_rotate_half_cols(rope), zq], axis=-1).reshape(Q_LORA, N_HEADS * LANES).astype(BF16)
    wuk = w_uk[0].reshape(KV_LORA, N_HEADS * NOPE_DIM).astype(BF16)
    wuv = w_uv[0].reshape(KV_LORA, N_HEADS * V_DIM).astype(BF16)
    wukt = w_uk[0].transpose(1, 2, 0).astype(BF16)
    wuv_h = w_uv[0].transpose(1, 0, 2).astype(BF16)
    wr = jnp.concatenate([w_group_router[0], w_router[0],
                          jnp.zeros((d, LANES - N_GROUPS - N_EXPERTS), F32)], axis=1)
    wrhi = wr.astype(BF16)
    wrlo = (wr - wrhi.astype(F32)).astype(BF16)
    br = jnp.concatenate([b_group_router[0], b_router[0],
                          jnp.zeros((LANES - N_GROUPS - N_EXPERTS,), F32)]).reshape(1, LANES)
    row = lambda g: g.reshape(1, -1)

    pos = jnp.concatenate([jnp.tile(jnp.arange(seq, dtype=jnp.int32), n_batch),
                           past_len + jnp.tile(jnp.arange(sd, dtype=jnp.int32), bd)])
    cs = _rope_table(pos)

    x_p, x_s = x_prompt.reshape(tp, d), x_sample.reshape(ts, d)
    u, kv, kr, q, k, v = _inproj(x_p, x_s, row(g_mix_norm[0]), wa, wkr, row(g_q_norm[0]), row(g_kv_norm[0]),
                                 wq, wqp, wuk, wuv, cs)

    u_s = u[tp:].reshape(bd, sd, POOL_DIM)
    slab = HALO + 8
    u_full_s = jnp.concatenate([state_pool[0].astype(F32), u_s,
                                jnp.zeros((bd, slab - POOL_STATE_LEN - sd, POOL_DIM), F32)], axis=1)
    pooled_p = _pool(u, tp, seq)
    pooled_s = _pool(u_full_s.reshape(bd * slab, POOL_DIM), bd * slab, None)
    pooled_s = pooled_s.reshape(bd, slab, POOL_DIM)[:, POOL_STATE_LEN:POOL_STATE_LEN + sd].reshape(ts, POOL_DIM)

    o_prompt = _flash(q, k, v, n_batch, seq)
    qa, qr = _qabs(q, wukt, tp, ts)
    qa = qa.reshape(bd, sd * N_HEADS, KV_LORA)
    qr = qr.reshape(bd, sd * N_HEADS, LANES)
    pad_new = lambda z: jnp.pad(z.reshape(bd, sd, -1), ((0, 0), (0, NEW_ROWS_PAD - sd), (0, 0)))
    o_lat = _decode(page_table, qa, qr, pad_new(kv[tp:]), pad_new(kr[tp:]), cache_kv_latent,
                    jnp.swapaxes(cache_k_rope, 2, 3))
    o_sample = _ov(o_lat.reshape(ts, N_HEADS * KV_LORA), wuv_h)

    x1, hn, ridx, rw = _post(x_p, x_s, pooled_p, pooled_s, o_prompt, o_sample, row(g_mix_norm[0]), wg,
                             w_pool[0].astype(BF16), row(pool_scale[0]), w_o[0].astype(BF16), row(g_ffn_norm[0]),
                             wrhi, wrlo, br)

    dest, slot_tok, block_expert, n_used = _dispatch(ridx[:, :TOP_K], MOE_BLOCK)
    ys = _moe(block_expert, n_used, slot_tok, hn, w_gate_e, w_up_e, w_down_e)
    y_p, y_s = _combine(dest.reshape(t // COMBINE_TILE, 1, TOP_K * COMBINE_TILE), x1, rw, row(g_final_norm), ys, tp)

    pool_p = jnp.stack([u[(i + 1) * seq - POOL_STATE_LEN:(i + 1) * seq] for i in range(n_batch)])
    pool_s = jnp.concatenate([state_pool[0].astype(F32), u_s], axis=1)[:, -POOL_STATE_LEN:]
    return (y_p.reshape(n_batch, seq, d), y_s.reshape(bd, sd, d),
            kv[:tp].reshape(1, n_batch, seq, KV_LORA), kr[:tp].reshape(1, n_batch, seq, ROPE_DIM),
            pool_p[None],
            kv[tp:].reshape(1, bd, sd, KV_LORA), kr[tp:].reshape(1, bd, sd, ROPE_DIM), pool_s[None])
```

```python
import functools

import jax
import jax.numpy as jnp
from jax import lax
from jax.experimental import pallas as pl
from jax.experimental.pallas import tpu as pltpu

F32 = jnp.float32
BF16 = jnp.bfloat16

D_MODEL = 2048
POOL_WINDOWS = (2, 4, 8, 16)
POOL_GROUP_DIM = 256
POOL_DIM = 1024
POOL_OUT_GROUP_DIM = 512
POOL_STATE_LEN = 15
N_HEADS = 16
V_DIM = 128
Q_LORA = 512
KV_LORA = 512
NOPE_DIM = 128
ROPE_DIM = 64
HALF_ROPE = ROPE_DIM // 2
ROPE_THETA = 10000.0
SOFTMAX_SCALE = (NOPE_DIM + ROPE_DIM) ** -0.5
SCORE_SCALE_LOG2 = SOFTMAX_SCALE * 1.4426950408889634
PAGE_SIZE = 128
N_GROUPS = 8
EXPERTS_PER_GROUP = 8
N_EXPERTS = 64
TOP_K = 2
D_EXPERT = 1408
EPS = 1e-6

LANES = 128
SUBROWS = D_MODEL // LANES
QK_DIM = 2 * LANES
HALO = 16
ROW_TILE = 256
POST_TILE = 128
ATTN_Q_TILE = 1024
ATTN_K_TILE = 512
DECODE_STREAMS = 1
PAGES_PER_STREAM = 32
DECODE_SLOTS = 4
NEW_ROWS_PAD = 16
MOE_BLOCK = 384
GATE_UP_ROWS = 256
DOWN_ROWS = 512
COMBINE_TILE = 128
WEIGHT_DMA_PRIORITY = 1
ROW_DMA_PRIORITY = 0
VMEM_LIMIT = 56 * 1024 * 1024


def _rms(xf, g):
    ms = jnp.mean(xf * xf, axis=-1, keepdims=True)
    return xf * lax.rsqrt(ms + EPS) * g


def _dot(a, b):
    return jnp.dot(a, b, preferred_element_type=F32)


def _dot_nt(a, b):
    return lax.dot_general(a, b, (((1,), (1,)), ((), ())), preferred_element_type=F32)


def _resident(shape):
    nd = len(shape)
    return pl.BlockSpec(shape, lambda *_: (0,) * nd, pipeline_mode=pl.Buffered(1))


def _params(**kw):
    return pltpu.CompilerParams(vmem_limit_bytes=VMEM_LIMIT, **kw)


def _split_rows(tm, width, n_first):
    return (pl.BlockSpec((tm, width), lambda i: (jnp.minimum(i, n_first - 1), 0)),
            pl.BlockSpec((tm, width), lambda i: (jnp.maximum(i - n_first, 0), 0)))


def _inproj_kernel(x_ref, xs_ref, gmix_ref, wa_ref, wkr_ref, gq_ref, gkv_ref, wq_ref, wqp_ref, wuk_ref, wuv_ref,
                   cs_ref, u_ref, kv_ref, kr_ref, q_ref, k_ref, v_ref, *, prompt_tiles):
    x = jnp.where(pl.program_id(0) < prompt_tiles, x_ref[...], xs_ref[...])
    xn = _rms(x, gmix_ref[...]).astype(BF16)
    u_ref[...] = _dot(xn, wa_ref[:, 0:POOL_DIM])
    ql = _rms(_dot(xn, wa_ref[:, POOL_DIM:POOL_DIM + Q_LORA]), gq_ref[...]).astype(BF16)
    kvf = _rms(_dot(xn, wa_ref[:, POOL_DIM + Q_LORA:]), gkv_ref[...])
    kv_ref[...] = kvf
    kvb = kvf.astype(BF16)
    cos = cs_ref[:, 0:LANES]
    sin = cs_ref[:, LANES:]
    hk = _dot(xn, wkr_ref[...])
    kr = hk[:, 0:LANES] * cos + hk[:, LANES:] * sin
    kr_ref[...] = kr[:, 0:ROPE_DIM]
    krb = kr.astype(BF16)
    for hp in range(N_HEADS // 2):
        qq = _dot(ql, wq_ref[:, hp * 2 * QK_DIM:(hp + 1) * 2 * QK_DIM])
        pp = _dot(ql, wqp_ref[:, hp * 2 * LANES:(hp + 1) * 2 * LANES])
        kn = _dot(kvb, wuk_ref[:, hp * 2 * NOPE_DIM:(hp + 1) * 2 * NOPE_DIM])
        vv = _dot(kvb, wuv_ref[:, hp * 2 * V_DIM:(hp + 1) * 2 * V_DIM])
        for s in range(2):
            h = 2 * hp + s
            qh = qq[:, s * QK_DIM:(s + 1) * QK_DIM]
            rope = qh[:, LANES:] * cos + pp[:, s * LANES:(s + 1) * LANES] * sin
            q_ref[h, :, 0:LANES] = qh[:, 0:LANES].astype(BF16)
            q_ref[h, :, LANES:] = rope.astype(BF16)
            k_ref[h, :, 0:LANES] = kn[:, s * NOPE_DIM:(s + 1) * NOPE_DIM].astype(BF16)
            k_ref[h, :, LANES:] = krb
            v_ref[h] = vv[:, s * V_DIM:(s + 1) * V_DIM].astype(BF16)


def _inproj(x_p, x_s, gmix, wa, wkr, gq, gkv, wq, wqp, wuk, wuv, cs):
    tm = ROW_TILE
    n_p = x_p.shape[0] // tm
    t = x_p.shape[0] + x_s.shape[0]
    row = lambda w: pl.BlockSpec((tm, w), lambda i: (i, 0))
    head = lambda w: pl.BlockSpec((N_HEADS, tm, w), lambda i: (0, i, 0))
    return pl.pallas_call(
        functools.partial(_inproj_kernel, prompt_tiles=n_p),
        grid=(t // tm,),
        in_specs=[*_split_rows(tm, D_MODEL, n_p), _resident(gmix.shape), _resident(wa.shape), _resident(wkr.shape),
                  _resident(gq.shape), _resident(gkv.shape), _resident(wq.shape), _resident(wqp.shape),
                  _resident(wuk.shape), _resident(wuv.shape), row(2 * LANES)],
        out_specs=[row(POOL_DIM), row(KV_LORA), row(ROPE_DIM), head(QK_DIM), head(QK_DIM), head(V_DIM)],
        out_shape=[jax.ShapeDtypeStruct((t, POOL_DIM), F32), jax.ShapeDtypeStruct((t, KV_LORA), F32),
                   jax.ShapeDtypeStruct((t, ROPE_DIM), F32), jax.ShapeDtypeStruct((N_HEADS, t, QK_DIM), BF16),
                   jax.ShapeDtypeStruct((N_HEADS, t, QK_DIM), BF16), jax.ShapeDtypeStruct((N_HEADS, t, V_DIM), BF16)],
        compiler_params=_params(),
        name="inproj",
    )(x_p, x_s, gmix, wa, wkr, gq, gkv, wq, wqp, wuk, wuv, cs)


def _pool_kernel(uprev_ref, u_ref, o_ref, buf_ref, *, tm, seq_len):
    i = pl.program_id(0)
    if seq_len is None:
        seq_start = i == 0
    else:
        pos0 = (i * tm) % seq_len
        seq_start = pos0 == 0
    buf_ref[0:HALO, :] = jnp.where(seq_start, 0.0, uprev_ref[...])
    buf_ref[HALO:HALO + tm, :] = u_ref[...]
    for gi, w in enumerate(POOL_WINDOWS):
        c0, c1 = gi * POOL_GROUP_DIM, (gi + 1) * POOL_GROUP_DIM
        u_new = buf_ref[HALO:HALO + tm, c0:c1]
        acc = u_new
        for j in range(1, w):
            acc = acc + buf_ref[HALO - j:HALO - j + tm, c0:c1]
        if seq_len is None:
            count = float(w)
        else:
            pos = pos0 + lax.broadcasted_iota(jnp.int32, (tm, 1), 0)
            count = jnp.minimum(w, pos + 1).astype(F32)
        o_ref[:, c0:c1] = acc / count - u_new


def _pool(u, n_rows, seq_len):
    tm = ROW_TILE
    per = tm // HALO
    return pl.pallas_call(
        functools.partial(_pool_kernel, tm=tm, seq_len=seq_len),
        grid=(n_rows // tm,),
        in_specs=[pl.BlockSpec((HALO, POOL_DIM), lambda i: (jnp.maximum(i * per - 1, 0), 0)),
                  pl.BlockSpec((tm, POOL_DIM), lambda i: (i, 0))],
        out_specs=pl.BlockSpec((tm, POOL_DIM), lambda i: (i, 0)),
        out_shape=jax.ShapeDtypeStruct((n_rows, POOL_DIM), F32),
        scratch_shapes=[pltpu.VMEM((HALO + tm, POOL_DIM), F32)],
        compiler_params=_params(),
        name="pool",
    )(u, u)


def _flash_kernel(q_ref, k_ref, v_ref, o_ref, *, tq, tk):
    qi = pl.program_id(2)
    q = q_ref[...]

    def step(kb, carry, diagonal):
        m, l, acc = carry
        off = pl.multiple_of(kb * tk, tk)
        s = _dot_nt(q, k_ref[pl.ds(off, tk), :]) * SCORE_SCALE_LOG2
        if diagonal:
            r = qi * tq + lax.broadcasted_iota(jnp.int32, (tq, tk), 0)
            c = off + lax.broadcasted_iota(jnp.int32, (tq, tk), 1)
            s = jnp.where(c <= r, s, -jnp.inf)
        m_new = jnp.maximum(m, jnp.max(s, axis=-1, keepdims=True))
        alpha = jnp.exp2(m - m_new)
        p = jnp.exp2(s - m_new)
        l = alpha * l + jnp.sum(p, axis=-1, keepdims=True)
        acc = alpha * acc + _dot(p.astype(BF16), v_ref[pl.ds(off, tk), :])
        return m_new, l, acc

    n_full = qi * (tq // tk)
    init = (jnp.full((tq, 1), -jnp.inf, F32), jnp.zeros((tq, 1), F32), jnp.zeros((tq, V_DIM), F32))
    carry = lax.fori_loop(0, n_full, lambda kb, c: step(kb, c, False), init)
    for d in range(tq // tk):
        carry = step(n_full + d, carry, True)
    _, l, acc = carry
    o_ref[...] = acc / l


def _flash(q, k, v, n_batch, seq):
    tq, tk = ATTN_Q_TILE, ATTN_K_TILE
    nq = seq // tq
    return pl.pallas_call(
        functools.partial(_flash_kernel, tq=tq, tk=tk),
        grid=(n_batch, N_HEADS, nq),
        in_specs=[pl.BlockSpec((None, tq, QK_DIM), lambda b, h, i: (h, b * nq + i, 0)),
                  pl.BlockSpec((None, seq, QK_DIM), lambda b, h, i: (h, b, 0)),
                  pl.BlockSpec((None, seq, V_DIM), lambda b, h, i: (h, b, 0))],
        out_specs=pl.BlockSpec((tq, V_DIM), lambda b, h, i: (b * nq + i, h)),
        out_shape=jax.ShapeDtypeStruct((n_batch * seq, N_HEADS * V_DIM), F32),
        compiler_params=_params(),
        name="flash",
    )(q, k, v)


def _qabs_kernel(q_ref, wukt_ref, qa_ref, qr_ref):
    qa_ref[...] = _dot(q_ref[:, 0:NOPE_DIM], wukt_ref[...]).astype(BF16)
    qr_ref[...] = q_ref[:, LANES:]


def _qabs(q, wukt, row0, n_rows):
    blk = row0 // n_rows
    return pl.pallas_call(
        _qabs_kernel,
        grid=(N_HEADS,),
        in_specs=[pl.BlockSpec((None, n_rows, QK_DIM), lambda h: (h, blk, 0)),
                  pl.BlockSpec((None, NOPE_DIM, KV_LORA), lambda h: (h, 0, 0))],
        out_specs=[pl.BlockSpec((n_rows, KV_LORA), lambda h: (0, h)),
                   pl.BlockSpec((n_rows, LANES), lambda h: (0, h))],
        out_shape=[jax.ShapeDtypeStruct((n_rows, N_HEADS * KV_LORA), BF16),
                   jax.ShapeDtypeStruct((n_rows, N_HEADS * LANES), BF16)],
        compiler_params=_params(),
        name="qabs",
    )(q, wukt)


def _decode_kernel(pt_ref, qa_ref, qr_ref, kvn_ref, krn_ref, kv_hbm, krt_hbm, o_ref,
                   kv_buf, krt_buf, m_ref, l_ref, acc_ref, kv_sem, kr_sem, *, n_chunks, heads):
    b = pl.program_id(0)
    n_seq = pl.num_programs(0)
    n_slots, cp = kv_buf.shape[0], kv_buf.shape[1]
    ahead = n_slots - 1
    per_stream = cp // DECODE_STREAMS

    def page_copies(g, slot, i):
        page = pt_ref[g // n_chunks, (g % n_chunks) * cp + i]
        return (pltpu.make_async_copy(kv_hbm.at[0, page], kv_buf.at[slot, i], kv_sem.at[slot]),
                pltpu.make_async_copy(krt_hbm.at[0, page], krt_buf.at[slot, i], kr_sem.at[slot]))

    def start_chunk(g, slot):
        for i in range(cp):
            kv_copy, kr_copy = page_copies(g, slot, i)
            kv_copy.start(priority=i % 2)
            kr_copy.start(priority=(i + 1) % 2)

    def wait_chunk(g, slot):
        for i in range(cp):
            for copy in page_copies(g, slot, i):
                copy.wait()

    @pl.when(b == 0)
    def _():
        for c in range(ahead):
            start_chunk(c, c)

    m_ref[...] = jnp.full(m_ref.shape, -jnp.inf, F32)
    l_ref[...] = jnp.zeros(l_ref.shape, F32)
    acc_ref[...] = jnp.zeros(acc_ref.shape, F32)
    qa = qa_ref[...]
    qr = qr_ref[:, 0:ROPE_DIM]

    def consume(slot):
        for t in range(DECODE_STREAMS):
            p0 = t * per_stream
            kvb = kv_buf[slot, p0:p0 + per_stream].reshape(per_stream * PAGE_SIZE, KV_LORA).astype(BF16)
            krt = jnp.concatenate([krt_buf[slot, p0 + i] for i in range(per_stream)], axis=1).astype(BF16)
            s = (_dot_nt(qa, kvb) + _dot(qr, krt)) * SCORE_SCALE_LOG2
            m = m_ref[t]
            m_new = jnp.maximum(m, jnp.max(s, axis=-1, keepdims=True))
            alpha = jnp.exp2(m - m_new)
            p = jnp.exp2(s - m_new)
            l_ref[t] = alpha * l_ref[t] + jnp.sum(p, axis=-1, keepdims=True)
            acc_ref[t] = alpha * acc_ref[t] + _dot(p.astype(BF16), kvb)
            m_ref[t] = m_new

    def ring_turn(cc, carry):
        for slot in range(n_slots):
            g = b * n_chunks + cc * n_slots + slot
            wait_chunk(g, slot)

            @pl.when(g + ahead < n_seq * n_chunks)
            def _():
                start_chunk(g + ahead, (slot + ahead) % n_slots)

            consume(slot)
        return carry

    lax.fori_loop(0, n_chunks // n_slots, ring_turn, 0)

    kvn = kvn_ref[...].astype(BF16)
    s_new = (_dot_nt(qa, kvn) + _dot_nt(qr, krn_ref[...].astype(BF16))) * SCORE_SCALE_LOG2
    tok = lax.broadcasted_iota(jnp.int32, s_new.shape, 0) // heads
    col = lax.broadcasted_iota(jnp.int32, s_new.shape, 1)
    s_new = jnp.where(col <= tok, s_new, -jnp.inf)
    m = jnp.max(s_new, axis=-1, keepdims=True)
    for t in range(DECODE_STREAMS):
        m = jnp.maximum(m, m_ref[t])
    p_new = jnp.exp2(s_new - m)
    l = jnp.sum(p_new, axis=-1, keepdims=True)
    acc = _dot(p_new.astype(BF16), kvn)
    for t in range(DECODE_STREAMS):
        w = jnp.exp2(m_ref[t] - m)
        l = l + w * l_ref[t]
        acc = acc + w * acc_ref[t]
    o_ref[...] = acc / l


def _decode(page_table, qa, qr, kvn, krn, cache_kv, cache_krt):
    bd, n_pages = page_table.shape
    rows = qa.shape[1]
    cp = DECODE_STREAMS * PAGES_PER_STREAM
    n_chunks = n_pages // cp
    ns = DECODE_SLOTS
    assert n_pages % cp == 0 and n_chunks % ns == 0
    per_seq = lambda r, w: pl.BlockSpec((None, r, w), lambda b, pt: (b, 0, 0))
    grid_spec = pltpu.PrefetchScalarGridSpec(
        num_scalar_prefetch=1,
        grid=(bd,),
        in_specs=[per_seq(rows, KV_LORA), per_seq(rows, LANES), per_seq(NEW_ROWS_PAD, KV_LORA),
                  per_seq(NEW_ROWS_PAD, ROPE_DIM), pl.BlockSpec(memory_space=pl.ANY),
                  pl.BlockSpec(memory_space=pl.ANY)],
        out_specs=per_seq(rows, KV_LORA),
        scratch_shapes=[pltpu.VMEM((ns, cp, PAGE_SIZE, KV_LORA), F32), pltpu.VMEM((ns, cp, ROPE_DIM, PAGE_SIZE), F32),
                        pltpu.VMEM((DECODE_STREAMS, rows, 1), F32), pltpu.VMEM((DECODE_STREAMS, rows, 1), F32),
                        pltpu.VMEM((DECODE_STREAMS, rows, KV_LORA), F32),
                        pltpu.SemaphoreType.DMA((ns,)), pltpu.SemaphoreType.DMA((ns,))],
    )
    return pl.pallas_call(
        functools.partial(_decode_kernel, n_chunks=n_chunks, heads=N_HEADS),
        grid_spec=grid_spec,
        out_shape=jax.ShapeDtypeStruct((bd, rows, KV_LORA), F32),
        compiler_params=_params(),
        name="decode",
    )(page_table, qa, qr, kvn, krn, cache_kv, cache_krt)


def _ov_kernel(olat_ref, wuv_ref, o_ref):
    o_ref[...] = _dot(olat_ref[...].astype(BF16), wuv_ref[...])


def _ov(olat2d, wuv_h):
    n_rows = olat2d.shape[0]
    return pl.pallas_call(
        _ov_kernel,
        grid=(N_HEADS,),
        in_specs=[pl.BlockSpec((n_rows, KV_LORA), lambda h: (0, h)),
                  pl.BlockSpec((None, KV_LORA, V_DIM), lambda h: (h, 0, 0))],
        out_specs=pl.BlockSpec((n_rows, V_DIM), lambda h: (0, h)),
        out_shape=jax.ShapeDtypeStruct((n_rows, N_HEADS * V_DIM), F32),
        compiler_params=_params(),
        name="ov",
    )(olat2d, wuv_h)


def _post_kernel(xp_ref, xs_ref, pp_ref, ps_ref, op_ref, os_ref, gmix_ref, wg_ref, wpool_ref, pscale_ref, wo_ref,
                 gffn_ref, wrhi_ref, wrlo_ref, br_ref, x1_ref, hn_ref, ridx_ref, rw_ref, merged_ref, *,
                 prompt_tiles):
    is_prompt = pl.program_id(0) < prompt_tiles
    x = jnp.where(is_prompt, xp_ref[...], xs_ref[...])
    xn = _rms(x, gmix_ref[...]).astype(BF16)
    w = POOL_OUT_GROUP_DIM
    for g in range(len(POOL_WINDOWS)):
        cols = slice(g * w, (g + 1) * w)
        pcols = slice(g * POOL_GROUP_DIM, (g + 1) * POOL_GROUP_DIM)
        pooled = jnp.where(is_prompt, pp_ref[:, pcols], ps_ref[:, pcols]).astype(BF16)
        y_pool = _dot(pooled, wpool_ref[g]) * pscale_ref[:, cols]
        gate_pool = jax.nn.sigmoid(_dot(xn, wg_ref[:, cols]))
        gate_mla = jax.nn.sigmoid(_dot(xn, wg_ref[:, D_MODEL + g * w:D_MODEL + (g + 1) * w]))
        y_mla = jnp.where(is_prompt, op_ref[:, cols], os_ref[:, cols])
        merged_ref[:, cols] = (gate_pool * y_pool + gate_mla * y_mla).astype(BF16)
    x1 = x + _dot(merged_ref[...], wo_ref[...])
    x1_ref[...] = x1
    hn = _rms(x1, gffn_ref[...])
    _store_token_major(hn_ref, hn)

    hi = hn.astype(BF16)
    lo = (hn - hi.astype(F32)).astype(BF16)
    logits = _dot(hi, wrhi_ref[...]) + (_dot(hi, wrlo_ref[...]) + _dot(lo, wrhi_ref[...])) + br_ref[...]

    lane = lax.broadcasted_iota(jnp.int32, logits.shape, 1)
    lane_f = lane.astype(F32)
    none = float(LANES)
    gl = jnp.where(lane < N_GROUPS, logits, -jnp.inf)
    g_max = jnp.max(gl, axis=-1, keepdims=True)
    g_sel = jnp.min(jnp.where(gl == g_max, lane_f, none), axis=-1, keepdims=True)
    g_w = 1.0 / jnp.sum(jnp.exp(gl - g_max), axis=-1, keepdims=True)
    first = N_GROUPS + EXPERTS_PER_GROUP * g_sel
    el = jnp.where((lane_f >= first) & (lane_f < first + EXPERTS_PER_GROUP), logits, -jnp.inf)
    t1 = jnp.max(el, axis=-1, keepdims=True)
    i1 = jnp.min(jnp.where(el == t1, lane_f, none), axis=-1, keepdims=True)
    el2 = jnp.where(lane_f == i1, -jnp.inf, el)
    t2 = jnp.max(el2, axis=-1, keepdims=True)
    i2 = jnp.min(jnp.where(el2 == t2, lane_f, none), axis=-1, keepdims=True)
    e2 = jnp.exp(t2 - t1)
    w1 = g_w / (1.0 + e2)
    w2 = g_w * e2 / (1.0 + e2)
    ridx = jnp.where(lane == 0, i1 - N_GROUPS, jnp.where(lane == 1, i2 - N_GROUPS, 0.0))
    ridx_ref[...] = ridx.astype(jnp.int32)
    rw_ref[...] = jnp.where(lane == 0, w1, jnp.where(lane == 1, w2, 0.0))


def _post(x_p, x_s, pooled_p, pooled_s, o_prompt, o_sample, gmix, wg, wpool, pscale, wo, gffn, wrhi, wrlo, br):
    t = x_p.shape[0] + x_s.shape[0]
    tm = POST_TILE
    n_p = x_p.shape[0] // tm
    row = lambda w: pl.BlockSpec((tm, w), lambda i: (i, 0))
    return pl.pallas_call(
        functools.partial(_post_kernel, prompt_tiles=n_p),
        grid=(t // tm,),
        in_specs=[*_split_rows(tm, D_MODEL, n_p), *_split_rows(tm, POOL_DIM, n_p), *_split_rows(tm, D_MODEL, n_p),
                  _resident(gmix.shape), _resident(wg.shape),
                  _resident(wpool.shape), _resident(pscale.shape), _resident(wo.shape), _resident(gffn.shape),
                  _resident(wrhi.shape), _resident(wrlo.shape), _resident(br.shape)],
        out_specs=[row(D_MODEL), pl.BlockSpec((tm * SUBROWS, LANES), lambda i: (i, 0)), row(LANES), row(LANES)],
        out_shape=[jax.ShapeDtypeStruct((t, D_MODEL), F32), jax.ShapeDtypeStruct((t * SUBROWS, LANES), F32),
                   jax.ShapeDtypeStruct((t, LANES), jnp.int32), jax.ShapeDtypeStruct((t, LANES), F32)],
        scratch_shapes=[pltpu.VMEM((tm, D_MODEL), BF16)],
        compiler_params=_params(),
        name="post",
    )(x_p, x_s, pooled_p, pooled_s, o_prompt, o_sample, gmix, wg, wpool, pscale, wo, gffn, wrhi, wrlo, br)


def _store_token_major(ref, x):
    rows = x.shape[0]
    for s in range(SUBROWS):
        ref[pl.ds(s, rows, stride=SUBROWS), :] = x[:, s * LANES:(s + 1) * LANES]


def _load_token_major(ref, rows):
    return jnp.concatenate([ref[pl.ds(s, rows, stride=SUBROWS), :] for s in range(SUBROWS)], axis=1)


def _row_copy(src_hbm, src_row, dst_ref, dst_row, sem):
    def token_rows(row):
        first = row * SUBROWS
        return pl.ds(first if isinstance(first, int) else pl.multiple_of(first, SUBROWS), SUBROWS)

    return pltpu.make_async_copy(src_hbm.at[token_rows(src_row), :], dst_ref.at[token_rows(dst_row), :], sem)


def _moe_kernel(be_ref, nbu_ref, tok_next_ref, tok_first_ref, hn_hbm, wg_hbm, wu_hbm, wd_hbm, o_ref,
                x_buf, act_ref, gu_buf, d_buf, x_sem, g_sem, u_sem, d_sem):
    b = pl.program_id(0)
    n_used = nbu_ref[0]
    bm = x_buf.shape[1] // SUBROWS
    kg = gu_buf.shape[2]
    n_gu = D_MODEL // kg
    kd = d_buf.shape[1]
    down_stages = [(r0, min(kd, D_EXPERT - r0)) for r0 in range(0, D_EXPERT, kd)]
    n_dn = len(down_stages)
    assert n_gu % 2 == 0 and bm % n_gu == 0

    def gate_up_copies(e, j, slot):
        rows = pl.ds(j * kg, kg)
        return (pltpu.make_async_copy(wg_hbm.at[0, e, rows, :], gu_buf.at[slot, 0], g_sem.at[slot]),
                pltpu.make_async_copy(wu_hbm.at[0, e, rows, :], gu_buf.at[slot, 1], u_sem.at[slot]))

    def down_copy(e, c, slot):
        r0, n = down_stages[c]
        return pltpu.make_async_copy(wd_hbm.at[0, e, pl.ds(r0, n), :], d_buf.at[slot, pl.ds(0, n), :], d_sem.at[slot])

    def start_gate_up(e, j, slot):
        for copy in gate_up_copies(e, j, slot):
            copy.start(priority=WEIGHT_DMA_PRIORITY)

    def start_rows(tok_ref, slot, r0, r1):
        for r in range(r0, r1):
            _row_copy(hn_hbm, tok_ref[0, 0, r], x_buf.at[slot], r, x_sem.at[slot]).start(priority=ROW_DMA_PRIORITY)

    def wait_rows(slot):
        def body(r, carry):
            _row_copy(hn_hbm, 0, x_buf.at[slot], r, x_sem.at[slot]).wait()
            return carry
        lax.fori_loop(0, bm, body, 0)

    @pl.when(b == 0)
    def _():
        start_gate_up(be_ref[0], 0, 0)
        start_gate_up(be_ref[0], 1, 1)

        def body(r, carry):
            _row_copy(hn_hbm, tok_first_ref[0, 0, r], x_buf.at[0], r, x_sem.at[0]).start(priority=ROW_DMA_PRIORITY)
            return carry
        lax.fori_loop(0, bm, body, 0)

    @pl.when(b < n_used)
    def _():
        e = be_ref[b]
        e_next = be_ref[b + 1]
        x_slot = b % 2
        down_copy(e, 0, 0).start(priority=WEIGHT_DMA_PRIORITY)
        down_copy(e, 1, 1).start(priority=WEIGHT_DMA_PRIORITY)
        wait_rows(x_slot)
        xb = _load_token_major(x_buf.at[x_slot], bm).astype(BF16)
        rows_per_stage = bm // n_gu

        def fetch_gate_up(j):
            slot = j % 2
            for copy in gate_up_copies(e, j, slot):
                copy.wait()
            w = (gu_buf[slot, 0].astype(BF16), gu_buf[slot, 1].astype(BF16))
            if j + 2 < n_gu:
                start_gate_up(e, j + 2, slot)
            else:
                start_gate_up(e_next, j + 2 - n_gu, slot)
            return w

        def fetch_down(c):
            slot = c % 2
            down_copy(e, c, slot).wait()
            w = d_buf[slot, 0:down_stages[c][1]].astype(BF16)
            if c + 2 < n_dn:
                down_copy(e, c + 2, slot).start(priority=WEIGHT_DMA_PRIORITY)
            return w

        w = fetch_gate_up(0)
        h_gate = h_up = None
        for j in range(n_gu):
            w_next = fetch_gate_up(j + 1) if j + 1 < n_gu else fetch_down(0)
            start_rows(tok_next_ref, 1 - x_slot, j * rows_per_stage, (j + 1) * rows_per_stage)
            xk = xb[:, j * kg:(j + 1) * kg]
            d_gate, d_up = _dot(xk, w[0]), _dot(xk, w[1])
            h_gate = d_gate if j == 0 else h_gate + d_gate
            h_up = d_up if j == 0 else h_up + d_up
            w = w_next
        act_ref[...] = (jax.nn.silu(h_gate) * h_up).astype(BF16)
        out = None
        for c in range(n_dn):
            w_next = fetch_down(c + 1) if c + 1 < n_dn else None
            r0, n = down_stages[c]
            part = _dot(act_ref[:, r0:r0 + n], w)
            out = part if c == 0 else out + part
            w = w_next
        _store_token_major(o_ref, out)

    @pl.when(b == n_used)
    def _():
        wait_rows(b % 2)
        for slot in range(2):
            for copy in gate_up_copies(be_ref[b], slot, slot):
                copy.wait()

    @pl.when(b >= n_used)
    def _():
        o_ref[...] = jnp.zeros(o_ref.shape, o_ref.dtype)


def _moe(block_expert, n_used, slot_tok, hn, w_gate, w_up, w_down):
    nb, _, bm = slot_tok.shape
    tok_spec = lambda index_map: pl.BlockSpec((1, 1, bm), index_map, memory_space=pltpu.SMEM)
    grid_spec = pltpu.PrefetchScalarGridSpec(
        num_scalar_prefetch=2,
        grid=(nb,),
        in_specs=[tok_spec(lambda b, be, nbu: (jnp.minimum(b + 1, nb - 1), 0, 0)),
                  tok_spec(lambda b, be, nbu: (0, 0, 0)),
                  pl.BlockSpec(memory_space=pl.ANY), pl.BlockSpec(memory_space=pl.ANY),
                  pl.BlockSpec(memory_space=pl.ANY), pl.BlockSpec(memory_space=pl.ANY)],
        out_specs=pl.BlockSpec((bm * SUBROWS, LANES), lambda b, be, nbu: (b, 0)),
        scratch_shapes=[pltpu.VMEM((2, bm * SUBROWS, LANES), F32), pltpu.VMEM((bm, D_EXPERT), BF16),
                        pltpu.VMEM((2, 2, GATE_UP_ROWS, D_EXPERT), F32), pltpu.VMEM((2, DOWN_ROWS, D_MODEL), F32),
                        pltpu.SemaphoreType.DMA((2,)), pltpu.SemaphoreType.DMA((2,)),
                        pltpu.SemaphoreType.DMA((2,)), pltpu.SemaphoreType.DMA((2,))],
    )
    return pl.pallas_call(
        _moe_kernel,
        grid_spec=grid_spec,
        out_shape=jax.ShapeDtypeStruct((nb * bm * SUBROWS, LANES), F32),
        compiler_params=_params(),
        name="moe",
    )(block_expert, n_used, slot_tok, slot_tok, hn, w_gate, w_up, w_down)


def _combine_kernel(pos_first_ref, pos_next_ref, x1_ref, rw_ref, gfin_ref, ys_hbm, yp_ref, ys_ref, buf_ref, sem, *,
                    prompt_tiles):
    i = pl.program_id(0)
    last = pl.num_programs(0) - 1
    n = x1_ref.shape[0]
    slot = i % 2

    def copy(pos_ref, dst_slot, r, k):
        return _row_copy(ys_hbm, pos_ref[0, 0, TOP_K * r + k], buf_ref.at[dst_slot, k], r, sem.at[dst_slot])

    def wait_slot(dst_slot):
        def body(r, carry):
            for k in range(TOP_K):
                copy(pos_next_ref, dst_slot, r, k).wait()
            return carry
        lax.fori_loop(0, n, body, 0)

    @pl.when(i == 0)
    def _():
        def body(r, carry):
            for k in range(TOP_K):
                copy(pos_first_ref, 0, r, k).start(priority=k)
            return carry
        lax.fori_loop(0, n, body, 0)

    wait_slot(slot)
    for r in range(n):
        for k in range(TOP_K):
            copy(pos_next_ref, 1 - slot, r, k).start(priority=k)
    rw = rw_ref[...]
    moe = (rw[:, 0:1] * _load_token_major(buf_ref.at[slot, 0], n)
           + rw[:, 1:2] * _load_token_major(buf_ref.at[slot, 1], n))
    y = _rms(x1_ref[...] + moe, gfin_ref[...])

    @pl.when(i == last)
    def _():
        wait_slot(1 - slot)

    is_prompt = i < prompt_tiles

    @pl.when(is_prompt)
    def _():
        yp_ref[...] = y

    @pl.when(jnp.logical_not(is_prompt))
    def _():
        ys_ref[...] = y


def _combine(pos, x1, rw, gfin, ys, n_prompt_rows):
    t = x1.shape[0]
    tm = COMBINE_TILE
    n_p = n_prompt_rows // tm
    n_tiles = t // tm
    pos_spec = lambda index_map: pl.BlockSpec((1, 1, TOP_K * tm), index_map, memory_space=pltpu.SMEM)
    return pl.pallas_call(
        functools.partial(_combine_kernel, prompt_tiles=n_p),
        grid=(n_tiles,),
        in_specs=[pos_spec(lambda i: (0, 0, 0)),
                  pos_spec(lambda i: (jnp.minimum(i + 1, n_tiles - 1), 0, 0)),
                  pl.BlockSpec((tm, D_MODEL), lambda i: (i, 0)),
                  pl.BlockSpec((tm, LANES), lambda i: (i, 0)),
                  pl.BlockSpec(gfin.shape, lambda i: (0, 0)),
                  pl.BlockSpec(memory_space=pl.ANY)],
        out_specs=list(_split_rows(tm, D_MODEL, n_p)),
        out_shape=[jax.ShapeDtypeStruct((n_prompt_rows, D_MODEL), F32),
                   jax.ShapeDtypeStruct((t - n_prompt_rows, D_MODEL), F32)],
        scratch_shapes=[pltpu.VMEM((2, TOP_K, tm * SUBROWS, LANES), F32), pltpu.SemaphoreType.DMA((2,))],
        compiler_params=_params(),
        name="combine",
    )(pos, pos, x1, rw, gfin, ys)


def _dispatch(experts, bm):
    t = experts.shape[0]
    a = t * TOP_K
    nb = -(-(a + N_EXPERTS * (bm - 1)) // bm) + 1
    e_flat = experts.reshape(a)
    onehot = (e_flat[:, None] == jnp.arange(N_EXPERTS, dtype=jnp.int32)[None, :]).astype(jnp.int32)
    csum = jnp.cumsum(onehot, axis=0)
    rank = jnp.sum(csum * onehot, axis=1) - 1
    counts = csum[-1]
    padded = (counts + bm - 1) // bm * bm
    pend = jnp.cumsum(padded)
    pstart = pend - padded
    dest = pstart[e_flat] + rank
    n_used = (pend[-1] // bm).astype(jnp.int32)
    blk = jnp.arange(nb, dtype=jnp.int32)
    be = jnp.minimum(jnp.searchsorted(pend, blk * bm, side='right'), N_EXPERTS - 1).astype(jnp.int32)
    be = jnp.where(blk < n_used, be, be[jnp.maximum(n_used - 1, 0)])
    slot_tok = jnp.zeros((nb * bm,), jnp.int32).at[dest].set(jnp.arange(a, dtype=jnp.int32) // TOP_K)
    return dest.astype(jnp.int32), slot_tok.reshape(nb, 1, bm), be, n_used.reshape(1)


def _rope_table(pos):
    inv = 1.0 / (ROPE_THETA ** (jnp.arange(0, ROPE_DIM, 2, dtype=F32) / ROPE_DIM))
    ang = pos.astype(F32)[:, None] * inv[None, :]
    cos, sin = jnp.cos(ang), jnp.sin(ang)
    z = jnp.zeros((pos.shape[0], LANES - ROPE_DIM), F32)
    return jnp.concatenate([cos, cos, z, sin, sin, z], axis=1)


def _rotate_half_cols(w):
    return jnp.concatenate([-w[..., HALF_ROPE:], w[..., :HALF_ROPE]], axis=-1)


def kernel(x_prompt, x_sample, state_pool, cache_kv_latent, cache_k_rope, page_table, g_mix_norm, w_in, g_q_norm,
           g_kv_norm, w_uq, w_uk, w_uv, w_pool, pool_scale, w_o, g_ffn_norm, w_group_router, b_group_router,
           w_router, b_router, w_gate_e, w_up_e, w_down_e, g_final_norm):
    n_batch, seq, d = x_prompt.shape
    bd, sd, _ = x_sample.shape
    assert w_in.shape[0] == 1 and d == D_MODEL, "single-layer step only"
    n_pages = page_table.shape[1]
    past_len = n_pages * PAGE_SIZE
    tp, ts = n_batch * seq, bd * sd
    t = tp + ts
    assert tp % ts == 0 and seq % ROW_TILE == 0 and t % ROW_TILE == 0 and ts % ROW_TILE == 0

    w_in0 = w_in[0]
    o_kr = POOL_DIM + Q_LORA + KV_LORA
    wa = w_in0[:, :o_kr].astype(BF16)
    w_kr = w_in0[:, o_kr:o_kr + ROPE_DIM]
    zk = jnp.zeros((d, LANES - ROPE_DIM), F32)
    wkr = jnp.concatenate([w_kr, zk, _rotate_half_cols(w_kr), zk], axis=1).astype(BF16)
    wg = w_in0[:, o_kr + ROPE_DIM:].astype(BF16)
    nope, rope = w_uq[0][:, :, :NOPE_DIM], w_uq[0][:, :, NOPE_DIM:]
    zq = jnp.zeros((Q_LORA, N_HEADS, LANES - ROPE_DIM), F32)
    wq = jnp.concatenate([nope, rope, zq], axis=-1).reshape(Q_LORA, N_HEADS * QK_DIM).astype(BF16)
    wqp = jnp.concatenate([_rotate_half_cols(rope), zq], axis=-1).reshape(Q_LORA, N_HEADS * LANES).astype(BF16)
    wuk = w_uk[0].reshape(KV_LORA, N_HEADS * NOPE_DIM).astype(BF16)
    wuv = w_uv[0].reshape(KV_LORA, N_HEADS * V_DIM).astype(BF16)
    wukt = w_uk[0].transpose(1, 2, 0).astype(BF16)
    wuv_h = w_uv[0].transpose(1, 0, 2).astype(BF16)
    wr = jnp.concatenate([w_group_router[0], w_router[0],
                          jnp.zeros((d, LANES - N_GROUPS - N_EXPERTS), F32)], axis=1)
    wrhi = wr.astype(BF16)
    wrlo = (wr - wrhi.astype(F32)).astype(BF16)
    br = jnp.concatenate([b_group_router[0], b_router[0],
                          jnp.zeros((LANES - N_GROUPS - N_EXPERTS,), F32)]).reshape(1, LANES)
    row = lambda g: g.reshape(1, -1)

    pos = jnp.concatenate([jnp.tile(jnp.arange(seq, dtype=jnp.int32), n_batch),
                           past_len + jnp.tile(jnp.arange(sd, dtype=jnp.int32), bd)])
    cs = _rope_table(pos)

    x_p, x_s = x_prompt.reshape(tp, d), x_sample.reshape(ts, d)
    u, kv, kr, q, k, v = _inproj(x_p, x_s, row(g_mix_norm[0]), wa, wkr, row(g_q_norm[0]), row(g_kv_norm[0]),
                                 wq, wqp, wuk, wuv, cs)

    u_s = u[tp:].reshape(bd, sd, POOL_DIM)
    slab = HALO + 8
    u_full_s = jnp.concatenate([state_pool[0].astype(F32), u_s,
                                jnp.zeros((bd, slab - POOL_STATE_LEN - sd, POOL_DIM), F32)], axis=1)
    pooled_p = _pool(u, tp, seq)
    pooled_s = _pool(u_full_s.reshape(bd * slab, POOL_DIM), bd * slab, None)
    pooled_s = pooled_s.reshape(bd, slab, POOL_DIM)[:, POOL_STATE_LEN:POOL_STATE_LEN + sd].reshape(ts, POOL_DIM)

    o_prompt = _flash(q, k, v, n_batch, seq)
    qa, qr = _qabs(q, wukt, tp, ts)
    qa = qa.reshape(bd, sd * N_HEADS, KV_LORA)
    qr = qr.reshape(bd, sd * N_HEADS, LANES)
    pad_new = lambda z: jnp.pad(z.reshape(bd, sd, -1), ((0, 0), (0, NEW_ROWS_PAD - sd), (0, 0)))
    o_lat = _decode(page_table, qa, qr, pad_new(kv[tp:]), pad_new(kr[tp:]), cache_kv_latent,
                    jnp.swapaxes(cache_k_rope, 2, 3))
    o_sample = _ov(o_lat.reshape(ts, N_HEADS * KV_LORA), wuv_h)

    x1, hn, ridx, rw = _post(x_p, x_s, pooled_p, pooled_s, o_prompt, o_sample, row(g_mix_norm[0]), wg,
                             w_pool[0].astype(BF16), row(pool_scale[0]), w_o[0].astype(BF16), row(g_ffn_norm[0]),
                             wrhi, wrlo, br)

    dest, slot_tok, block_expert, n_used = _dispatch(ridx[:, :TOP_K], MOE_BLOCK)
    ys = _moe(block_expert, n_used, slot_tok, hn, w_gate_e, w_up_e, w_down_e)
    y_p, y_s = _combine(dest.reshape(t // COMBINE_TILE, 1, TOP_K * COMBINE_TILE), x1, rw, row(g_final_norm), ys, tp)

    pool_p = jnp.stack([u[(i + 1) * seq - POOL_STATE_LEN:(i + 1) * seq] for i in range(n_batch)])
    pool_s = jnp.concatenate([state_pool[0].astype(F32), u_s], axis=1)[:, -POOL_STATE_LEN:]
    return (y_p.reshape(n_batch, seq, d), y_s.reshape(bd, sd, d),
            kv[:tp].reshape(1, n_batch, seq, KV_LORA), kr[:tp].reshape(1, n_batch, seq, ROPE_DIM),
            pool_p[None],
            kv[tp:].reshape(1, bd, sd, KV_LORA), kr[tp:].reshape(1, bd, sd, ROPE_DIM), pool_s[None])
```

```python
import functools

import jax
import jax.numpy as jnp
from jax import lax
from jax.experimental import pallas as pl
from jax.experimental.pallas import tpu as pltpu

F32 = jnp.float32
BF16 = jnp.bfloat16

D_MODEL = 2048
POOL_WINDOWS = (2, 4, 8, 16)
POOL_GROUP_DIM = 256
POOL_DIM = 1024
POOL_OUT_GROUP_DIM = 512
POOL_STATE_LEN = 15
N_HEADS = 16
V_DIM = 128
Q_LORA = 512
KV_LORA = 512
NOPE_DIM = 128
ROPE_DIM = 64
HALF_ROPE = ROPE_DIM // 2
ROPE_THETA = 10000.0
SOFTMAX_SCALE = (NOPE_DIM + ROPE_DIM) ** -0.5
SCORE_SCALE_LOG2 = SOFTMAX_SCALE * 1.4426950408889634
PAGE_SIZE = 128
N_GROUPS = 8
EXPERTS_PER_GROUP = 8
N_EXPERTS = 64
TOP_K = 2
D_EXPERT = 1408
EPS = 1e-6

LANES = 128
SUBROWS = D_MODEL // LANES
QK_DIM = 2 * LANES
HALO = 16
ROW_TILE = 256
POST_TILE = 128
ATTN_Q_TILE = 1024
ATTN_K_TILE = 512
DECODE_STREAMS = 1
PAGES_PER_STREAM = 32
DECODE_SLOTS = 4
NEW_ROWS_PAD = 16
MOE_BLOCK = 384
GATE_UP_ROWS = 256
GATE_UP_RING = 3
DOWN_ROWS = 512
COMBINE_TILE = 128
VMEM_LIMIT = 56 * 1024 * 1024


def _rms(xf, g):
    ms = jnp.mean(xf * xf, axis=-1, keepdims=True)
    return xf * lax.rsqrt(ms + EPS) * g


def _dot(a, b):
    return jnp.dot(a, b, preferred_element_type=F32)


def _dot_nt(a, b):
    return lax.dot_general(a, b, (((1,), (1,)), ((), ())), preferred_element_type=F32)


def _resident(shape):
    nd = len(shape)
    return pl.BlockSpec(shape, lambda *_: (0,) * nd, pipeline_mode=pl.Buffered(1))


def _params(**kw):
    return pltpu.CompilerParams(vmem_limit_bytes=VMEM_LIMIT, **kw)


def _split_rows(tm, width, n_first):
    return (pl.BlockSpec((tm, width), lambda i: (jnp.minimum(i, n_first - 1), 0)),
            pl.BlockSpec((tm, width), lambda i: (jnp.maximum(i - n_first, 0), 0)))


def _inproj_kernel(x_ref, xs_ref, gmix_ref, wa_ref, wkr_ref, gq_ref, gkv_ref, wq_ref, wqp_ref, wuk_ref, wuv_ref,
                   cs_ref, u_ref, kv_ref, kr_ref, q_ref, k_ref, v_ref, *, prompt_tiles):
    x = jnp.where(pl.program_id(0) < prompt_tiles, x_ref[...], xs_ref[...])
    xn = _rms(x, gmix_ref[...]).astype(BF16)
    u_ref[...] = _dot(xn, wa_ref[:, 0:POOL_DIM])
    ql = _rms(_dot(xn, wa_ref[:, POOL_DIM:POOL_DIM + Q_LORA]), gq_ref[...]).astype(BF16)
    kvf = _rms(_dot(xn, wa_ref[:, POOL_DIM + Q_LORA:]), gkv_ref[...])
    kv_ref[...] = kvf
    kvb = kvf.astype(BF16)
    cos = cs_ref[:, 0:LANES]
    sin = cs_ref[:, LANES:]
    hk = _dot(xn, wkr_ref[...])
    kr = hk[:, 0:LANES] * cos + hk[:, LANES:] * sin
    kr_ref[...] = kr[:, 0:ROPE_DIM]
    krb = kr.astype(BF16)
    for hp in range(N_HEADS // 2):
        qq = _dot(ql, wq_ref[:, hp * 2 * QK_DIM:(hp + 1) * 2 * QK_DIM])
        pp = _dot(ql, wqp_ref[:, hp * 2 * LANES:(hp + 1) * 2 * LANES])
        kn = _dot(kvb, wuk_ref[:, hp * 2 * NOPE_DIM:(hp + 1) * 2 * NOPE_DIM])
        vv = _dot(kvb, wuv_ref[:, hp * 2 * V_DIM:(hp + 1) * 2 * V_DIM])
        for s in range(2):
            h = 2 * hp + s
            qh = qq[:, s * QK_DIM:(s + 1) * QK_DIM]
            rope = qh[:, LANES:] * cos + pp[:, s * LANES:(s + 1) * LANES] * sin
            q_ref[h, :, 0:LANES] = qh[:, 0:LANES].astype(BF16)
            q_ref[h, :, LANES:] = rope.astype(BF16)
            k_ref[h, :, 0:LANES] = kn[:, s * NOPE_DIM:(s + 1) * NOPE_DIM].astype(BF16)
            k_ref[h, :, LANES:] = krb
            v_ref[h] = vv[:, s * V_DIM:(s + 1) * V_DIM].astype(BF16)


def _inproj(x_p, x_s, gmix, wa, wkr, gq, gkv, wq, wqp, wuk, wuv, cs):
    tm = ROW_TILE
    n_p = x_p.shape[0] // tm
    t = x_p.shape[0] + x_s.shape[0]
    row = lambda w: pl.BlockSpec((tm, w), lambda i: (i, 0))
    head = lambda w: pl.BlockSpec((N_HEADS, tm, w), lambda i: (0, i, 0))
    return pl.pallas_call(
        functools.partial(_inproj_kernel, prompt_tiles=n_p),
        grid=(t // tm,),
        in_specs=[*_split_rows(tm, D_MODEL, n_p), _resident(gmix.shape), _resident(wa.shape), _resident(wkr.shape),
                  _resident(gq.shape), _resident(gkv.shape), _resident(wq.shape), _resident(wqp.shape),
                  _resident(wuk.shape), _resident(wuv.shape), row(2 * LANES)],
        out_specs=[row(POOL_DIM), row(KV_LORA), row(ROPE_DIM), head(QK_DIM), head(QK_DIM), head(V_DIM)],
        out_shape=[jax.ShapeDtypeStruct((t, POOL_DIM), F32), jax.ShapeDtypeStruct((t, KV_LORA), F32),
                   jax.ShapeDtypeStruct((t, ROPE_DIM), F32), jax.ShapeDtypeStruct((N_HEADS, t, QK_DIM), BF16),
                   jax.ShapeDtypeStruct((N_HEADS, t, QK_DIM), BF16), jax.ShapeDtypeStruct((N_HEADS, t, V_DIM), BF16)],
        compiler_params=_params(),
        name="inproj",
    )(x_p, x_s, gmix, wa, wkr, gq, gkv, wq, wqp, wuk, wuv, cs)


def _pool_kernel(uprev_ref, u_ref, o_ref, buf_ref, *, tm, seq_len):
    i = pl.program_id(0)
    if seq_len is None:
        seq_start = i == 0
    else:
        pos0 = (i * tm) % seq_len
        seq_start = pos0 == 0
    buf_ref[0:HALO, :] = jnp.where(seq_start, 0.0, uprev_ref[...])
    buf_ref[HALO:HALO + tm, :] = u_ref[...]
    for gi, w in enumerate(POOL_WINDOWS):
        c0, c1 = gi * POOL_GROUP_DIM, (gi + 1) * POOL_GROUP_DIM
        u_new = buf_ref[HALO:HALO + tm, c0:c1]
        acc = u_new
        for j in range(1, w):
            acc = acc + buf_ref[HALO - j:HALO - j + tm, c0:c1]
        if seq_len is None:
            count = float(w)
        else:
            pos = pos0 + lax.broadcasted_iota(jnp.int32, (tm, 1), 0)
            count = jnp.minimum(w, pos + 1).astype(F32)
        o_ref[:, c0:c1] = acc / count - u_new


def _pool(u, n_rows, seq_len):
    tm = ROW_TILE
    per = tm // HALO
    return pl.pallas_call(
        functools.partial(_pool_kernel, tm=tm, seq_len=seq_len),
        grid=(n_rows // tm,),
        in_specs=[pl.BlockSpec((HALO, POOL_DIM), lambda i: (jnp.maximum(i * per - 1, 0), 0)),
                  pl.BlockSpec((tm, POOL_DIM), lambda i: (i, 0))],
        out_specs=pl.BlockSpec((tm, POOL_DIM), lambda i: (i, 0)),
        out_shape=jax.ShapeDtypeStruct((n_rows, POOL_DIM), F32),
        scratch_shapes=[pltpu.VMEM((HALO + tm, POOL_DIM), F32)],
        compiler_params=_params(),
        name="pool",
    )(u, u)


def _flash_kernel(q_ref, k_ref, v_ref, o_ref, *, tq, tk):
    qi = pl.program_id(2)
    q = q_ref[...]

    def step(kb, carry, diagonal):
        m, l, acc = carry
        off = pl.multiple_of(kb * tk, tk)
        s = _dot_nt(q, k_ref[pl.ds(off, tk), :]) * SCORE_SCALE_LOG2
        if diagonal:
            r = qi * tq + lax.broadcasted_iota(jnp.int32, (tq, tk), 0)
            c = off + lax.broadcasted_iota(jnp.int32, (tq, tk), 1)
            s = jnp.where(c <= r, s, -jnp.inf)
        m_new = jnp.maximum(m, jnp.max(s, axis=-1, keepdims=True))
        alpha = jnp.exp2(m - m_new)
        p = jnp.exp2(s - m_new)
        l = alpha * l + jnp.sum(p, axis=-1, keepdims=True)
        acc = alpha * acc + _dot(p.astype(BF16), v_ref[pl.ds(off, tk), :])
        return m_new, l, acc

    n_full = qi * (tq // tk)
    init = (jnp.full((tq, 1), -jnp.inf, F32), jnp.zeros((tq, 1), F32), jnp.zeros((tq, V_DIM), F32))
    carry = lax.fori_loop(0, n_full, lambda kb, c: step(kb, c, False), init)
    for d in range(tq // tk):
        carry = step(n_full + d, carry, True)
    _, l, acc = carry
    o_ref[...] = acc / l


def _flash(q, k, v, n_batch, seq):
    tq, tk = ATTN_Q_TILE, ATTN_K_TILE
    nq = seq // tq
    return pl.pallas_call(
        functools.partial(_flash_kernel, tq=tq, tk=tk),
        grid=(n_batch, N_HEADS, nq),
        in_specs=[pl.BlockSpec((None, tq, QK_DIM), lambda b, h, i: (h, b * nq + i, 0)),
                  pl.BlockSpec((None, seq, QK_DIM), lambda b, h, i: (h, b, 0)),
                  pl.BlockSpec((None, seq, V_DIM), lambda b, h, i: (h, b, 0))],
        out_specs=pl.BlockSpec((tq, V_DIM), lambda b, h, i: (b * nq + i, h)),
        out_shape=jax.ShapeDtypeStruct((n_batch * seq, N_HEADS * V_DIM), F32),
        compiler_params=_params(),
        name="flash",
    )(q, k, v)


def _qabs_kernel(q_ref, wukt_ref, qa_ref, qr_ref):
    qa_ref[...] = _dot(q_ref[:, 0:NOPE_DIM], wukt_ref[...]).astype(BF16)
    qr_ref[...] = q_ref[:, LANES:]


def _qabs(q, wukt, row0, n_rows):
    blk = row0 // n_rows
    return pl.pallas_call(
        _qabs_kernel,
        grid=(N_HEADS,),
        in_specs=[pl.BlockSpec((None, n_rows, QK_DIM), lambda h: (h, blk, 0)),
                  pl.BlockSpec((None, NOPE_DIM, KV_LORA), lambda h: (h, 0, 0))],
        out_specs=[pl.BlockSpec((n_rows, KV_LORA), lambda h: (0, h)),
                   pl.BlockSpec((n_rows, LANES), lambda h: (0, h))],
        out_shape=[jax.ShapeDtypeStruct((n_rows, N_HEADS * KV_LORA), BF16),
                   jax.ShapeDtypeStruct((n_rows, N_HEADS * LANES), BF16)],
        compiler_params=_params(),
        name="qabs",
    )(q, wukt)


def _decode_kernel(pt_ref, qa_ref, qr_ref, kvn_ref, krn_ref, kv_hbm, krt_hbm, o_ref,
                   kv_buf, krt_buf, m_ref, l_ref, acc_ref, kv_sem, kr_sem, *, n_chunks, heads):
    b = pl.program_id(0)
    n_seq = pl.num_programs(0)
    n_slots, cp = kv_buf.shape[0], kv_buf.shape[1]
    ahead = n_slots - 1
    per_stream = cp // DECODE_STREAMS

    def page_copies(g, slot, i):
        page = pt_ref[g // n_chunks, (g % n_chunks) * cp + i]
        return (pltpu.make_async_copy(kv_hbm.at[0, page], kv_buf.at[slot, i], kv_sem.at[slot]),
                pltpu.make_async_copy(krt_hbm.at[0, page], krt_buf.at[slot, i], kr_sem.at[slot]))

    def start_chunk(g, slot):
        for i in range(cp):
            kv_copy, kr_copy = page_copies(g, slot, i)
            kv_copy.start(priority=i % 2)
            kr_copy.start(priority=(i + 1) % 2)

    def wait_chunk(g, slot):
        for i in range(cp):
            for copy in page_copies(g, slot, i):
                copy.wait()

    @pl.when(b == 0)
    def _():
        for c in range(ahead):
            start_chunk(c, c)

    m_ref[...] = jnp.full(m_ref.shape, -jnp.inf, F32)
    l_ref[...] = jnp.zeros(l_ref.shape, F32)
    acc_ref[...] = jnp.zeros(acc_ref.shape, F32)
    qa = qa_ref[...]
    qr = qr_ref[:, 0:ROPE_DIM]

    def consume(slot):
        for t in range(DECODE_STREAMS):
            p0 = t * per_stream
            kvb = kv_buf[slot, p0:p0 + per_stream].reshape(per_stream * PAGE_SIZE, KV_LORA).astype(BF16)
            krt = jnp.concatenate([krt_buf[slot, p0 + i] for i in range(per_stream)], axis=1).astype(BF16)
            s = (_dot_nt(qa, kvb) + _dot(qr, krt)) * SCORE_SCALE_LOG2
            m = m_ref[t]
            m_new = jnp.maximum(m, jnp.max(s, axis=-1, keepdims=True))
            alpha = jnp.exp2(m - m_new)
            p = jnp.exp2(s - m_new)
            l_ref[t] = alpha * l_ref[t] + jnp.sum(p, axis=-1, keepdims=True)
            acc_ref[t] = alpha * acc_ref[t] + _dot(p.astype(BF16), kvb)
            m_ref[t] = m_new

    def ring_turn(cc, carry):
        for slot in range(n_slots):
            g = b * n_chunks + cc * n_slots + slot
            wait_chunk(g, slot)

            @pl.when(g + ahead < n_seq * n_chunks)
            def _():
                start_chunk(g + ahead, (slot + ahead) % n_slots)

            consume(slot)
        return carry

    lax.fori_loop(0, n_chunks // n_slots, ring_turn, 0)

    kvn = kvn_ref[...].astype(BF16)
    s_new = (_dot_nt(qa, kvn) + _dot_nt(qr, krn_ref[...].astype(BF16))) * SCORE_SCALE_LOG2
    tok = lax.broadcasted_iota(jnp.int32, s_new.shape, 0) // heads
    col = lax.broadcasted_iota(jnp.int32, s_new.shape, 1)
    s_new = jnp.where(col <= tok, s_new, -jnp.inf)
    m = jnp.max(s_new, axis=-1, keepdims=True)
    for t in range(DECODE_STREAMS):
        m = jnp.maximum(m, m_ref[t])
    p_new = jnp.exp2(s_new - m)
    l = jnp.sum(p_new, axis=-1, keepdims=True)
    acc = _dot(p_new.astype(BF16), kvn)
    for t in range(DECODE_STREAMS):
        w = jnp.exp2(m_ref[t] - m)
        l = l + w * l_ref[t]
        acc = acc + w * acc_ref[t]
    o_ref[...] = acc / l


def _decode(page_table, qa, qr, kvn, krn, cache_kv, cache_krt):
    bd, n_pages = page_table.shape
    rows = qa.shape[1]
    cp = DECODE_STREAMS * PAGES_PER_STREAM
    n_chunks = n_pages // cp
    ns = DECODE_SLOTS
    assert n_pages % cp == 0 and n_chunks % ns == 0
    per_seq = lambda r, w: pl.BlockSpec((None, r, w), lambda b, pt: (b, 0, 0))
    grid_spec = pltpu.PrefetchScalarGridSpec(
        num_scalar_prefetch=1,
        grid=(bd,),
        in_specs=[per_seq(rows, KV_LORA), per_seq(rows, LANES), per_seq(NEW_ROWS_PAD, KV_LORA),
                  per_seq(NEW_ROWS_PAD, ROPE_DIM), pl.BlockSpec(memory_space=pl.ANY),
                  pl.BlockSpec(memory_space=pl.ANY)],
        out_specs=per_seq(rows, KV_LORA),
        scratch_shapes=[pltpu.VMEM((ns, cp, PAGE_SIZE, KV_LORA), F32), pltpu.VMEM((ns, cp, ROPE_DIM, PAGE_SIZE), F32),
                        pltpu.VMEM((DECODE_STREAMS, rows, 1), F32), pltpu.VMEM((DECODE_STREAMS, rows, 1), F32),
                        pltpu.VMEM((DECODE_STREAMS, rows, KV_LORA), F32),
                        pltpu.SemaphoreType.DMA((ns,)), pltpu.SemaphoreType.DMA((ns,))],
    )
    return pl.pallas_call(
        functools.partial(_decode_kernel, n_chunks=n_chunks, heads=N_HEADS),
        grid_spec=grid_spec,
        out_shape=jax.ShapeDtypeStruct((bd, rows, KV_LORA), F32),
        compiler_params=_params(),
        name="decode",
    )(page_table, qa, qr, kvn, krn, cache_kv, cache_krt)


def _ov_kernel(olat_ref, wuv_ref, o_ref):
    o_ref[...] = _dot(olat_ref[...].astype(BF16), wuv_ref[...])


def _ov(olat2d, wuv_h):
    n_rows = olat2d.shape[0]
    return pl.pallas_call(
        _ov_kernel,
        grid=(N_HEADS,),
        in_specs=[pl.BlockSpec((n_rows, KV_LORA), lambda h: (0, h)),
                  pl.BlockSpec((None, KV_LORA, V_DIM), lambda h: (h, 0, 0))],
        out_specs=pl.BlockSpec((n_rows, V_DIM), lambda h: (0, h)),
        out_shape=jax.ShapeDtypeStruct((n_rows, N_HEADS * V_DIM), F32),
        compiler_params=_params(),
        name="ov",
    )(olat2d, wuv_h)


def _post_kernel(xp_ref, xs_ref, pp_ref, ps_ref, op_ref, os_ref, gmix_ref, wg_ref, wpool_ref, pscale_ref, wo_ref,
                 gffn_ref, wrhi_ref, wrlo_ref, br_ref, x1_ref, hn_ref, ridx_ref, rw_ref, merged_ref, *,
                 prompt_tiles):
    is_prompt = pl.program_id(0) < prompt_tiles
    x = jnp.where(is_prompt, xp_ref[...], xs_ref[...])
    xn = _rms(x, gmix_ref[...]).astype(BF16)
    w = POOL_OUT_GROUP_DIM
    for g in range(len(POOL_WINDOWS)):
        cols = slice(g * w, (g + 1) * w)
        pcols = slice(g * POOL_GROUP_DIM, (g + 1) * POOL_GROUP_DIM)
        pooled = jnp.where(is_prompt, pp_ref[:, pcols], ps_ref[:, pcols]).astype(BF16)
        y_pool = _dot(pooled, wpool_ref[g]) * pscale_ref[:, cols]
        gate_pool = jax.nn.sigmoid(_dot(xn, wg_ref[:, cols]))
        gate_mla = jax.nn.sigmoid(_dot(xn, wg_ref[:, D_MODEL + g * w:D_MODEL + (g + 1) * w]))
        y_mla = jnp.where(is_prompt, op_ref[:, cols], os_ref[:, cols])
        merged_ref[:, cols] = (gate_pool * y_pool + gate_mla * y_mla).astype(BF16)
    x1 = x + _dot(merged_ref[...], wo_ref[...])
    x1_ref[...] = x1
    hn = _rms(x1, gffn_ref[...])
    _store_token_major(hn_ref, hn)

    hi = hn.astype(BF16)
    lo = (hn - hi.astype(F32)).astype(BF16)
    logits = _dot(hi, wrhi_ref[...]) + (_dot(hi, wrlo_ref[...]) + _dot(lo, wrhi_ref[...])) + br_ref[...]

    lane = lax.broadcasted_iota(jnp.int32, logits.shape, 1)
    lane_f = lane.astype(F32)
    none = float(LANES)
    gl = jnp.where(lane < N_GROUPS, logits, -jnp.inf)
    g_max = jnp.max(gl, axis=-1, keepdims=True)
    g_sel = jnp.min(jnp.where(gl == g_max, lane_f, none), axis=-1, keepdims=True)
    g_w = 1.0 / jnp.sum(jnp.exp(gl - g_max), axis=-1, keepdims=True)
    first = N_GROUPS + EXPERTS_PER_GROUP * g_sel
    el = jnp.where((lane_f >= first) & (lane_f < first + EXPERTS_PER_GROUP), logits, -jnp.inf)
    t1 = jnp.max(el, axis=-1, keepdims=True)
    i1 = jnp.min(jnp.where(el == t1, lane_f, none), axis=-1, keepdims=True)
    el2 = jnp.where(lane_f == i1, -jnp.inf, el)
    t2 = jnp.max(el2, axis=-1, keepdims=True)
    i2 = jnp.min(jnp.where(el2 == t2, lane_f, none), axis=-1, keepdims=True)
    e2 = jnp.exp(t2 - t1)
    w1 = g_w / (1.0 + e2)
    w2 = g_w * e2 / (1.0 + e2)
    ridx = jnp.where(lane == 0, i1 - N_GROUPS, jnp.where(lane == 1, i2 - N_GROUPS, 0.0))
    ridx_ref[...] = ridx.astype(jnp.int32)
    rw_ref[...] = jnp.where(lane == 0, w1, jnp.where(lane == 1, w2, 0.0))


def _post(x_p, x_s, pooled_p, pooled_s, o_prompt, o_sample, gmix, wg, wpool, pscale, wo, gffn, wrhi, wrlo, br):
    t = x_p.shape[0] + x_s.shape[0]
    tm = POST_TILE
    n_p = x_p.shape[0] // tm
    row = lambda w: pl.BlockSpec((tm, w), lambda i: (i, 0))
    return pl.pallas_call(
        functools.partial(_post_kernel, prompt_tiles=n_p),
        grid=(t // tm,),
        in_specs=[*_split_rows(tm, D_MODEL, n_p), *_split_rows(tm, POOL_DIM, n_p), *_split_rows(tm, D_MODEL, n_p),
                  _resident(gmix.shape), _resident(wg.shape),
                  _resident(wpool.shape), _resident(pscale.shape), _resident(wo.shape), _resident(gffn.shape),
                  _resident(wrhi.shape), _resident(wrlo.shape), _resident(br.shape)],
        out_specs=[row(D_MODEL), pl.BlockSpec((tm * SUBROWS, LANES), lambda i: (i, 0)), row(LANES), row(LANES)],
        out_shape=[jax.ShapeDtypeStruct((t, D_MODEL), F32), jax.ShapeDtypeStruct((t * SUBROWS, LANES), F32),
                   jax.ShapeDtypeStruct((t, LANES), jnp.int32), jax.ShapeDtypeStruct((t, LANES), F32)],
        scratch_shapes=[pltpu.VMEM((tm, D_MODEL), BF16)],
        compiler_params=_params(),
        name="post",
    )(x_p, x_s, pooled_p, pooled_s, o_prompt, o_sample, gmix, wg, wpool, pscale, wo, gffn, wrhi, wrlo, br)


def _store_token_major(ref, x):
    rows = x.shape[0]
    for s in range(SUBROWS):
        ref[pl.ds(s, rows, stride=SUBROWS), :] = x[:, s * LANES:(s + 1) * LANES]


def _load_token_major(ref, rows):
    return jnp.concatenate([ref[pl.ds(s, rows, stride=SUBROWS), :] for s in range(SUBROWS)], axis=1)


def _row_copy(src_hbm, src_row, dst_ref, dst_row, sem):
    def token_rows(row):
        first = row * SUBROWS
        return pl.ds(first if isinstance(first, int) else pl.multiple_of(first, SUBROWS), SUBROWS)

    return pltpu.make_async_copy(src_hbm.at[token_rows(src_row), :], dst_ref.at[token_rows(dst_row), :], sem)


def _moe_kernel(be_ref, nbu_ref, tok_next_ref, tok_first_ref, hn_hbm, wg_hbm, wu_hbm, wd_hbm, o_ref,
                x_buf, act_ref, gu_buf, d_buf, x_sem, g_sem, u_sem, d_sem):
    b = pl.program_id(0)
    n_used = nbu_ref[0]
    bm = x_buf.shape[1] // SUBROWS
    kg = gu_buf.shape[2]
    n_gu = D_MODEL // kg
    kd = d_buf.shape[1]
    down_stages = [(r0, min(kd, D_EXPERT - r0)) for r0 in range(0, D_EXPERT, kd)]
    n_dn = len(down_stages)
    assert bm % n_gu == 0

    def gate_up_copies(e, j, slot):
        half = kg // 2
        copies = []
        for m, (w_hbm, sem) in enumerate(((wg_hbm, g_sem), (wu_hbm, u_sem))):
            for h in range(2):
                src = w_hbm.at[0, e, pl.ds(j * kg + h * half, half), :]
                copies.append((pltpu.make_async_copy(src, gu_buf.at[slot, m, pl.ds(h * half, half), :], sem.at[slot]),
                               (m + h) % 2))
        return copies

    def down_copies(e, c, slot):
        r0, n = down_stages[c]
        half = n // 2
        return [(pltpu.make_async_copy(wd_hbm.at[0, e, pl.ds(r0 + h * half, half), :],
                                       d_buf.at[slot, pl.ds(h * half, half), :], d_sem.at[slot]), h) for h in range(2)]

    def start_all(copies):
        for copy, priority in copies:
            copy.start(priority=priority)

    def wait_all(copies):
        for copy, _ in copies:
            copy.wait()

    def start_gate_up(e, j, slot):
        start_all(gate_up_copies(e, j, slot))

    def start_rows(tok_ref, slot, r0, r1):
        for r in range(r0, r1):
            _row_copy(hn_hbm, tok_ref[0, 0, r], x_buf.at[slot], r, x_sem.at[slot]).start(priority=r % 2)

    def wait_rows(slot):
        def body(r, carry):
            _row_copy(hn_hbm, 0, x_buf.at[slot], r, x_sem.at[slot]).wait()
            return carry
        lax.fori_loop(0, bm, body, 0)

    ring = gu_buf.shape[0]
    first_slot = lax.rem(b * n_gu, ring)

    def gu_slot(j):
        return lax.rem(first_slot + j, ring)

    @pl.when(b == 0)
    def _():
        start_gate_up(be_ref[0], 0, 0)
        start_gate_up(be_ref[0], 1, 1)

        def body(r, carry):
            _row_copy(hn_hbm, tok_first_ref[0, 0, r], x_buf.at[0], r, x_sem.at[0]).start()
            return carry
        lax.fori_loop(0, bm, body, 0)

    @pl.when(b < n_used)
    def _():
        e = be_ref[b]
        e_next = be_ref[b + 1]
        x_slot = b % 2
        start_all(down_copies(e, 0, 0))
        start_all(down_copies(e, 1, 1))
        wait_rows(x_slot)
        xb = _load_token_major(x_buf.at[x_slot], bm).astype(BF16)
        rows_per_stage = bm // n_gu

        def fetch_gate_up(j):
            slot = gu_slot(j)
            wait_all(gate_up_copies(e, j, slot))
            if j + 2 < n_gu:
                start_gate_up(e, j + 2, gu_slot(j + 2))
            else:
                start_gate_up(e_next, j + 2 - n_gu, gu_slot(j + 2))
            return gu_buf[slot, 0].astype(BF16), gu_buf[slot, 1].astype(BF16)

        def fetch_down(c):
            slot = c % 2
            wait_all(down_copies(e, c, slot))
            w = d_buf[slot, 0:down_stages[c][1]].astype(BF16)
            if c + 2 < n_dn:
                start_all(down_copies(e, c + 2, slot))
            return w

        w = fetch_gate_up(0)
        h_gate = h_up = None
        for j in range(n_gu):
            w_next = fetch_gate_up(j + 1) if j + 1 < n_gu else fetch_down(0)
            start_rows(tok_next_ref, 1 - x_slot, j * rows_per_stage, (j + 1) * rows_per_stage)
            xk = xb[:, j * kg:(j + 1) * kg]
            d_gate, d_up = _dot(xk, w[0]), _dot(xk, w[1])
            h_gate = d_gate if j == 0 else h_gate + d_gate
            h_up = d_up if j == 0 else h_up + d_up
            w = w_next
        act_ref[...] = (jax.nn.silu(h_gate) * h_up).astype(BF16)
        out = None
        for c in range(n_dn):
            w_next = fetch_down(c + 1) if c + 1 < n_dn else None
            r0, n = down_stages[c]
            part = _dot(act_ref[:, r0:r0 + n], w)
            out = part if c == 0 else out + part
            w = w_next
        _store_token_major(o_ref, out)

    @pl.when(b == n_used)
    def _():
        wait_rows(b % 2)
        for j in range(2):
            wait_all(gate_up_copies(be_ref[b], j, gu_slot(j)))

    @pl.when(b >= n_used)
    def _():
        o_ref[...] = jnp.zeros(o_ref.shape, o_ref.dtype)


def _moe(block_expert, n_used, slot_tok, hn, w_gate, w_up, w_down):
    nb, _, bm = slot_tok.shape
    tok_spec = lambda index_map: pl.BlockSpec((1, 1, bm), index_map, memory_space=pltpu.SMEM)
    grid_spec = pltpu.PrefetchScalarGridSpec(
        num_scalar_prefetch=2,
        grid=(nb,),
        in_specs=[tok_spec(lambda b, be, nbu: (jnp.minimum(b + 1, nb - 1), 0, 0)),
                  tok_spec(lambda b, be, nbu: (0, 0, 0)),
                  pl.BlockSpec(memory_space=pl.ANY), pl.BlockSpec(memory_space=pl.ANY),
                  pl.BlockSpec(memory_space=pl.ANY), pl.BlockSpec(memory_space=pl.ANY)],
        out_specs=pl.BlockSpec((bm * SUBROWS, LANES), lambda b, be, nbu: (b, 0)),
        scratch_shapes=[pltpu.VMEM((2, bm * SUBROWS, LANES), F32), pltpu.VMEM((bm, D_EXPERT), BF16),
                        pltpu.VMEM((GATE_UP_RING, 2, GATE_UP_ROWS, D_EXPERT), F32),
                        pltpu.VMEM((2, DOWN_ROWS, D_MODEL), F32),
                        pltpu.SemaphoreType.DMA((2,)), pltpu.SemaphoreType.DMA((GATE_UP_RING,)),
                        pltpu.SemaphoreType.DMA((GATE_UP_RING,)), pltpu.SemaphoreType.DMA((2,))],
    )
    return pl.pallas_call(
        _moe_kernel,
        grid_spec=grid_spec,
        out_shape=jax.ShapeDtypeStruct((nb * bm * SUBROWS, LANES), F32),
        compiler_params=_params(),
        name="moe",
    )(block_expert, n_used, slot_tok, slot_tok, hn, w_gate, w_up, w_down)


def _combine_kernel(pos_first_ref, pos_next_ref, x1_ref, rw_ref, gfin_ref, ys_hbm, yp_ref, ys_ref, buf_ref, sem, *,
                    prompt_tiles):
    i = pl.program_id(0)
    last = pl.num_programs(0) - 1
    n = x1_ref.shape[0]
    slot = i % 2

    def copy(pos_ref, dst_slot, r, k):
        return _row_copy(ys_hbm, pos_ref[0, 0, TOP_K * r + k], buf_ref.at[dst_slot, k], r, sem.at[dst_slot])

    def wait_slot(dst_slot):
        def body(r, carry):
            for k in range(TOP_K):
                copy(pos_next_ref, dst_slot, r, k).wait()
            return carry
        lax.fori_loop(0, n, body, 0)

    @pl.when(i == 0)
    def _():
        def body(r, carry):
            for k in range(TOP_K):
                copy(pos_first_ref, 0, r, k).start(priority=k)
            return carry
        lax.fori_loop(0, n, body, 0)

    wait_slot(slot)
    for r in range(n):
        for k in range(TOP_K):
            copy(pos_next_ref, 1 - slot, r, k).start(priority=k)
    rw = rw_ref[...]
    moe = (rw[:, 0:1] * _load_token_major(buf_ref.at[slot, 0], n)
           + rw[:, 1:2] * _load_token_major(buf_ref.at[slot, 1], n))
    y = _rms(x1_ref[...] + moe, gfin_ref[...])

    @pl.when(i == last)
    def _():
        wait_slot(1 - slot)

    is_prompt = i < prompt_tiles

    @pl.when(is_prompt)
    def _():
        yp_ref[...] = y

    @pl.when(jnp.logical_not(is_prompt))
    def _():
        ys_ref[...] = y


def _combine(pos, x1, rw, gfin, ys, n_prompt_rows):
    t = x1.shape[0]
    tm = COMBINE_TILE
    n_p = n_prompt_rows // tm
    n_tiles = t // tm
    pos_spec = lambda index_map: pl.BlockSpec((1, 1, TOP_K * tm), index_map, memory_space=pltpu.SMEM)
    return pl.pallas_call(
        functools.partial(_combine_kernel, prompt_tiles=n_p),
        grid=(n_tiles,),
        in_specs=[pos_spec(lambda i: (0, 0, 0)),
                  pos_spec(lambda i: (jnp.minimum(i + 1, n_tiles - 1), 0, 0)),
                  pl.BlockSpec((tm, D_MODEL), lambda i: (i, 0)),
                  pl.BlockSpec((tm, LANES), lambda i: (i, 0)),
                  pl.BlockSpec(gfin.shape, lambda i: (0, 0)),
                  pl.BlockSpec(memory_space=pl.ANY)],
        out_specs=list(_split_rows(tm, D_MODEL, n_p)),
        out_shape=[jax.ShapeDtypeStruct((n_prompt_rows, D_MODEL), F32),
                   jax.ShapeDtypeStruct((t - n_prompt_rows, D_MODEL), F32)],
        scratch_shapes=[pltpu.VMEM((2, TOP_K, tm * SUBROWS, LANES), F32), pltpu.SemaphoreType.DMA((2,))],
        compiler_params=_params(),
        name="combine",
    )(pos, pos, x1, rw, gfin, ys)


def _dispatch(experts, bm):
    t = experts.shape[0]
    a = t * TOP_K
    nb = -(-(a + N_EXPERTS * (bm - 1)) // bm) + 1
    e_flat = experts.reshape(a)
    onehot = (e_flat[:, None] == jnp.arange(N_EXPERTS, dtype=jnp.int32)[None, :]).astype(jnp.int32)
    csum = jnp.cumsum(onehot, axis=0)
    rank = jnp.sum(csum * onehot, axis=1) - 1
    counts = csum[-1]
    padded = (counts + bm - 1) // bm * bm
    pend = jnp.cumsum(padded)
    pstart = pend - padded
    dest = pstart[e_flat] + rank
    n_used = (pend[-1] // bm).astype(jnp.int32)
    blk = jnp.arange(nb, dtype=jnp.int32)
    be = jnp.minimum(jnp.searchsorted(pend, blk * bm, side='right'), N_EXPERTS - 1).astype(jnp.int32)
    be = jnp.where(blk < n_used, be, be[jnp.maximum(n_used - 1, 0)])
    slot_tok = jnp.zeros((nb * bm,), jnp.int32).at[dest].set(jnp.arange(a, dtype=jnp.int32) // TOP_K)
    return dest.astype(jnp.int32), slot_tok.reshape(nb, 1, bm), be, n_used.reshape(1)


def _rope_table(pos):
    inv = 1.0 / (ROPE_THETA ** (jnp.arange(0, ROPE_DIM, 2, dtype=F32) / ROPE_DIM))
    ang = pos.astype(F32)[:, None] * inv[None, :]
    cos, sin = jnp.cos(ang), jnp.sin(ang)
    z = jnp.zeros((pos.shape[0], LANES - ROPE_DIM), F32)
    return jnp.concatenate([cos, cos, z, sin, sin, z], axis=1)


def _rotate_half_cols(w):
    return jnp.concatenate([-w[..., HALF_ROPE:], w[..., :HALF_ROPE]], axis=-1)


def kernel(x_prompt, x_sample, state_pool, cache_kv_latent, cache_k_rope, page_table, g_mix_norm, w_in, g_q_norm,
           g_kv_norm, w_uq, w_uk, w_uv, w_pool, pool_scale, w_o, g_ffn_norm, w_group_router, b_group_router,
           w_router, b_router, w_gate_e, w_up_e, w_down_e, g_final_norm):
    n_batch, seq, d = x_prompt.shape
    bd, sd, _ = x_sample.shape
    assert w_in.shape[0] == 1 and d == D_MODEL, "single-layer step only"
    n_pages = page_table.shape[1]
    past_len = n_pages * PAGE_SIZE
    tp, ts = n_batch * seq, bd * sd
    t = tp + ts
    assert tp % ts == 0 and seq % ROW_TILE == 0 and t % ROW_TILE == 0 and ts % ROW_TILE == 0

    w_in0 = w_in[0]
    o_kr = POOL_DIM + Q_LORA + KV_LORA
    wa = w_in0[:, :o_kr].astype(BF16)
    w_kr = w_in0[:, o_kr:o_kr + ROPE_DIM]
    zk = jnp.zeros((d, LANES - ROPE_DIM), F32)
    wkr = jnp.concatenate([w_kr, zk, _rotate_half_cols(w_kr), zk], axis=1).astype(BF16)
    wg = w_in0[:, o_kr + ROPE_DIM:].astype(BF16)
    nope, rope = w_uq[0][:, :, :NOPE_DIM], w_uq[0][:, :, NOPE_DIM:]
    zq = jnp.zeros((Q_LORA, N_HEADS, LANES - ROPE_DIM), F32)
    wq = jnp.concatenate([nope, rope, zq], axis=-1).reshape(Q_LORA, N_HEADS * QK_DIM).astype(BF16)
    wqp = jnp.concatenate([_rotate_half_cols(rope), zq], axis=-1).reshape(Q_LORA, N_HEADS * LANES).astype(BF16)
    wuk = w_uk[0].reshape(KV_LORA, N_HEADS * NOPE_DIM).astype(BF16)
    wuv = w_uv[0].reshape(KV_LORA, N_HEADS * V_DIM).astype(BF16)
    wukt = w_uk[0].transpose(1, 2, 0).astype(BF16)
    wuv_h = w_uv[0].transpose(1, 0, 2).astype(BF16)
    wr = jnp.concatenate([w_group_router[0], w_router[0],
                          jnp.zeros((d, LANES - N_GROUPS - N_EXPERTS), F32)], axis=1)
    wrhi = wr.astype(BF16)
    wrlo = (wr - wrhi.astype(F32)).astype(BF16)
    br = jnp.concatenate([b_group_router[0], b_router[0],
                          jnp.zeros((LANES - N_GROUPS - N_EXPERTS,), F32)]).reshape(1, LANES)
    row = lambda g: g.reshape(1, -1)

    pos = jnp.concatenate([jnp.tile(jnp.arange(seq, dtype=jnp.int32), n_batch),
                           past_len + jnp.tile(jnp.arange(sd, dtype=jnp.int32), bd)])
    cs = _rope_table(pos)

    x_p, x_s = x_prompt.reshape(tp, d), x_sample.reshape(ts, d)
    u, kv, kr, q, k, v = _inproj(x_p, x_s, row(g_mix_norm[0]), wa, wkr, row(g_q_norm[0]), row(g_kv_norm[0]),
                                 wq, wqp, wuk, wuv, cs)

    u_s = u[tp:].reshape(bd, sd, POOL_DIM)
    slab = HALO + 8
    u_full_s = jnp.concatenate([state_pool[0].astype(F32), u_s,
                                jnp.zeros((bd, slab - POOL_STATE_LEN - sd, POOL_DIM), F32)], axis=1)
    pooled_p = _pool(u, tp, seq)
    pooled_s = _pool(u_full_s.reshape(bd * slab, POOL_DIM), bd * slab, None)
    pooled_s = pooled_s.reshape(bd, slab, POOL_DIM)[:, POOL_STATE_LEN:POOL_STATE_LEN + sd].reshape(ts, POOL_DIM)

    o_prompt = _flash(q, k, v, n_batch, seq)
    qa, qr = _qabs(q, wukt, tp, ts)
    qa = qa.reshape(bd, sd * N_HEADS, KV_LORA)
    qr = qr.reshape(bd, sd * N_HEADS, LANES)
    pad_new = lambda z: jnp.pad(z.reshape(bd, sd, -1), ((0, 0), (0, NEW_ROWS_PAD - sd), (0, 0)))
    o_lat = _decode(page_table, qa, qr, pad_new(kv[tp:]), pad_new(kr[tp:]), cache_kv_latent,
                    jnp.swapaxes(cache_k_rope, 2, 3))
    o_sample = _ov(o_lat.reshape(ts, N_HEADS * KV_LORA), wuv_h)

    x1, hn, ridx, rw = _post(x_p, x_s, pooled_p, pooled_s, o_prompt, o_sample, row(g_mix_norm[0]), wg,
                             w_pool[0].astype(BF16), row(pool_scale[0]), w_o[0].astype(BF16), row(g_ffn_norm[0]),
                             wrhi, wrlo, br)

    dest, slot_tok, block_expert, n_used = _dispatch(ridx[:, :TOP_K], MOE_BLOCK)
    ys = _moe(block_expert, n_used, slot_tok, hn, w_gate_e, w_up_e, w_down_e)
    y_p, y_s = _combine(dest.reshape(t // COMBINE_TILE, 1, TOP_K * COMBINE_TILE), x1, rw, row(g_final_norm), ys, tp)

    pool_p = jnp.stack([u[(i + 1) * seq - POOL_STATE_LEN:(i + 1) * seq] for i in range(n_batch)])
    pool_s = jnp.concatenate([state_pool[0].astype(F32), u_s], axis=1)[:, -POOL_STATE_LEN:]
    return (y_p.reshape(n_batch, seq, d), y_s.reshape(bd, sd, d),
            kv[:tp].reshape(1, n_batch, seq, KV_LORA), kr[:tp].reshape(1, n_batch, seq, ROPE_DIM),
            pool_p[None],
            kv[tp:].reshape(1, bd, sd, KV_LORA), kr[tp:].reshape(1, bd, sd, ROPE_DIM), pool_s[None])
```

```python
import functools

import jax
import jax.numpy as jnp
from jax import lax
from jax.experimental import pallas as pl
from jax.experimental.pallas import tpu as pltpu

F32 = jnp.float32
BF16 = jnp.bfloat16

D_MODEL = 2048
POOL_WINDOWS = (2, 4, 8, 16)
POOL_GROUP_DIM = 256
POOL_DIM = 1024
POOL_OUT_GROUP_DIM = 512
POOL_STATE_LEN = 15
N_HEADS = 16
V_DIM = 128
Q_LORA = 512
KV_LORA = 512
NOPE_DIM = 128
ROPE_DIM = 64
HALF_ROPE = ROPE_DIM // 2
ROPE_THETA = 10000.0
SOFTMAX_SCALE = (NOPE_DIM + ROPE_DIM) ** -0.5
SCORE_SCALE_LOG2 = SOFTMAX_SCALE * 1.4426950408889634
PAGE_SIZE = 128
N_GROUPS = 8
EXPERTS_PER_GROUP = 8
N_EXPERTS = 64
TOP_K = 2
D_EXPERT = 1408
EPS = 1e-6

LANES = 128
SUBROWS = D_MODEL // LANES
QK_DIM = 2 * LANES
HALO = 16
ROW_TILE = 256
POST_TILE = 128
ATTN_Q_TILE = 1024
ATTN_K_TILE = 512
DECODE_STREAMS = 1
PAGES_PER_STREAM = 32
DECODE_SLOTS = 4
NEW_ROWS_PAD = 16
MOE_BLOCK = 384
GATE_UP_ROWS = 256
GATE_UP_RING = 3
DOWN_ROWS = 512
COMBINE_TILE = 128
VMEM_LIMIT = 56 * 1024 * 1024


def _rms(xf, g):
    ms = jnp.mean(xf * xf, axis=-1, keepdims=True)
    return xf * lax.rsqrt(ms + EPS) * g


def _dot(a, b):
    return jnp.dot(a, b, preferred_element_type=F32)


def _dot_nt(a, b):
    return lax.dot_general(a, b, (((1,), (1,)), ((), ())), preferred_element_type=F32)


def _resident(shape):
    nd = len(shape)
    return pl.BlockSpec(shape, lambda *_: (0,) * nd, pipeline_mode=pl.Buffered(1))


def _params(**kw):
    return pltpu.CompilerParams(vmem_limit_bytes=VMEM_LIMIT, **kw)


def _split_rows(tm, width, n_first):
    return (pl.BlockSpec((tm, width), lambda i: (jnp.minimum(i, n_first - 1), 0)),
            pl.BlockSpec((tm, width), lambda i: (jnp.maximum(i - n_first, 0), 0)))


def _inproj_kernel(x_ref, xs_ref, gmix_ref, wa_ref, wkr_ref, gq_ref, gkv_ref, wq_ref, wqp_ref, wuk_ref, wuv_ref,
                   cs_ref, u_ref, kv_ref, kr_ref, q_ref, k_ref, v_ref, *, prompt_tiles):
    x = jnp.where(pl.program_id(0) < prompt_tiles, x_ref[...], xs_ref[...])
    xn = _rms(x, gmix_ref[...]).astype(BF16)
    u_ref[...] = _dot(xn, wa_ref[:, 0:POOL_DIM])
    ql = _rms(_dot(xn, wa_ref[:, POOL_DIM:POOL_DIM + Q_LORA]), gq_ref[...]).astype(BF16)
    kvf = _rms(_dot(xn, wa_ref[:, POOL_DIM + Q_LORA:]), gkv_ref[...])
    kv_ref[...] = kvf
    kvb = kvf.astype(BF16)
    cos = cs_ref[:, 0:LANES]
    sin = cs_ref[:, LANES:]
    hk = _dot(xn, wkr_ref[...])
    kr = hk[:, 0:LANES] * cos + hk[:, LANES:] * sin
    kr_ref[...] = kr[:, 0:ROPE_DIM]
    krb = kr.astype(BF16)
    for hp in range(N_HEADS // 2):
        qq = _dot(ql, wq_ref[:, hp * 2 * QK_DIM:(hp + 1) * 2 * QK_DIM])
        pp = _dot(ql, wqp_ref[:, hp * 2 * LANES:(hp + 1) * 2 * LANES])
        kn = _dot(kvb, wuk_ref[:, hp * 2 * NOPE_DIM:(hp + 1) * 2 * NOPE_DIM])
        vv = _dot(kvb, wuv_ref[:, hp * 2 * V_DIM:(hp + 1) * 2 * V_DIM])
        for s in range(2):
            h = 2 * hp + s
            qh = qq[:, s * QK_DIM:(s + 1) * QK_DIM]
            rope = qh[:, LANES:] * cos + pp[:, s * LANES:(s + 1) * LANES] * sin
            q_ref[h, :, 0:LANES] = qh[:, 0:LANES].astype(BF16)
            q_ref[h, :, LANES:] = rope.astype(BF16)
            k_ref[h, :, 0:LANES] = kn[:, s * NOPE_DIM:(s + 1) * NOPE_DIM].astype(BF16)
            k_ref[h, :, LANES:] = krb
            v_ref[h] = vv[:, s * V_DIM:(s + 1) * V_DIM].astype(BF16)


def _inproj(x_p, x_s, gmix, wa, wkr, gq, gkv, wq, wqp, wuk, wuv, cs):
    tm = ROW_TILE
    n_p = x_p.shape[0] // tm
    t = x_p.shape[0] + x_s.shape[0]
    row = lambda w: pl.BlockSpec((tm, w), lambda i: (i, 0))
    head = lambda w: pl.BlockSpec((N_HEADS, tm, w), lambda i: (0, i, 0))
    return pl.pallas_call(
        functools.partial(_inproj_kernel, prompt_tiles=n_p),
        grid=(t // tm,),
        in_specs=[*_split_rows(tm, D_MODEL, n_p), _resident(gmix.shape), _resident(wa.shape), _resident(wkr.shape),
                  _resident(gq.shape), _resident(gkv.shape), _resident(wq.shape), _resident(wqp.shape),
                  _resident(wuk.shape), _resident(wuv.shape), row(2 * LANES)],
        out_specs=[row(POOL_DIM), row(KV_LORA), row(ROPE_DIM), head(QK_DIM), head(QK_DIM), head(V_DIM)],
        out_shape=[jax.ShapeDtypeStruct((t, POOL_DIM), F32), jax.ShapeDtypeStruct((t, KV_LORA), F32),
                   jax.ShapeDtypeStruct((t, ROPE_DIM), F32), jax.ShapeDtypeStruct((N_HEADS, t, QK_DIM), BF16),
                   jax.ShapeDtypeStruct((N_HEADS, t, QK_DIM), BF16), jax.ShapeDtypeStruct((N_HEADS, t, V_DIM), BF16)],
        compiler_params=_params(),
        name="inproj",
    )(x_p, x_s, gmix, wa, wkr, gq, gkv, wq, wqp, wuk, wuv, cs)


def _pool_kernel(uprev_ref, u_ref, o_ref, buf_ref, *, tm, seq_len):
    i = pl.program_id(0)
    if seq_len is None:
        seq_start = i == 0
    else:
        pos0 = (i * tm) % seq_len
        seq_start = pos0 == 0
    buf_ref[0:HALO, :] = jnp.where(seq_start, 0.0, uprev_ref[...])
    buf_ref[HALO:HALO + tm, :] = u_ref[...]
    for gi, w in enumerate(POOL_WINDOWS):
        c0, c1 = gi * POOL_GROUP_DIM, (gi + 1) * POOL_GROUP_DIM
        u_new = buf_ref[HALO:HALO + tm, c0:c1]
        acc = u_new
        for j in range(1, w):
            acc = acc + buf_ref[HALO - j:HALO - j + tm, c0:c1]
        if seq_len is None:
            count = float(w)
        else:
            pos = pos0 + lax.broadcasted_iota(jnp.int32, (tm, 1), 0)
            count = jnp.minimum(w, pos + 1).astype(F32)
        o_ref[:, c0:c1] = acc / count - u_new


def _pool(u, n_rows, seq_len):
    tm = ROW_TILE
    per = tm // HALO
    return pl.pallas_call(
        functools.partial(_pool_kernel, tm=tm, seq_len=seq_len),
        grid=(n_rows // tm,),
        in_specs=[pl.BlockSpec((HALO, POOL_DIM), lambda i: (jnp.maximum(i * per - 1, 0), 0)),
                  pl.BlockSpec((tm, POOL_DIM), lambda i: (i, 0))],
        out_specs=pl.BlockSpec((tm, POOL_DIM), lambda i: (i, 0)),
        out_shape=jax.ShapeDtypeStruct((n_rows, POOL_DIM), F32),
        scratch_shapes=[pltpu.VMEM((HALO + tm, POOL_DIM), F32)],
        compiler_params=_params(),
        name="pool",
    )(u, u)


def _flash_kernel(q_ref, k_ref, v_ref, o_ref, *, tq, tk):
    qi = pl.program_id(2)
    q = q_ref[...]

    def step(kb, carry, diagonal):
        m, l, acc = carry
        off = pl.multiple_of(kb * tk, tk)
        s = _dot_nt(q, k_ref[pl.ds(off, tk), :]) * SCORE_SCALE_LOG2
        if diagonal:
            r = qi * tq + lax.broadcasted_iota(jnp.int32, (tq, tk), 0)
            c = off + lax.broadcasted_iota(jnp.int32, (tq, tk), 1)
            s = jnp.where(c <= r, s, -jnp.inf)
        m_new = jnp.maximum(m, jnp.max(s, axis=-1, keepdims=True))
        alpha = jnp.exp2(m - m_new)
        p = jnp.exp2(s - m_new)
        l = alpha * l + jnp.sum(p, axis=-1, keepdims=True)
        acc = alpha * acc + _dot(p.astype(BF16), v_ref[pl.ds(off, tk), :])
        return m_new, l, acc

    n_full = qi * (tq // tk)
    init = (jnp.full((tq, 1), -jnp.inf, F32), jnp.zeros((tq, 1), F32), jnp.zeros((tq, V_DIM), F32))
    carry = lax.fori_loop(0, n_full, lambda kb, c: step(kb, c, False), init)
    for d in range(tq // tk):
        carry = step(n_full + d, carry, True)
    _, l, acc = carry
    o_ref[...] = acc / l


def _flash(q, k, v, n_batch, seq):
    tq, tk = ATTN_Q_TILE, ATTN_K_TILE
    nq = seq // tq
    return pl.pallas_call(
        functools.partial(_flash_kernel, tq=tq, tk=tk),
        grid=(n_batch, N_HEADS, nq),
        in_specs=[pl.BlockSpec((None, tq, QK_DIM), lambda b, h, i: (h, b * nq + i, 0)),
                  pl.BlockSpec((None, seq, QK_DIM), lambda b, h, i: (h, b, 0)),
                  pl.BlockSpec((None, seq, V_DIM), lambda b, h, i: (h, b, 0))],
        out_specs=pl.BlockSpec((tq, V_DIM), lambda b, h, i: (b * nq + i, h)),
        out_shape=jax.ShapeDtypeStruct((n_batch * seq, N_HEADS * V_DIM), F32),
        compiler_params=_params(),
        name="flash",
    )(q, k, v)


def _qabs_kernel(q_ref, wukt_ref, qa_ref, qr_ref):
    qa_ref[...] = _dot(q_ref[:, 0:NOPE_DIM], wukt_ref[...]).astype(BF16)
    qr_ref[...] = q_ref[:, LANES:]


def _qabs(q, wukt, row0, n_rows):
    blk = row0 // n_rows
    return pl.pallas_call(
        _qabs_kernel,
        grid=(N_HEADS,),
        in_specs=[pl.BlockSpec((None, n_rows, QK_DIM), lambda h: (h, blk, 0)),
                  pl.BlockSpec((None, NOPE_DIM, KV_LORA), lambda h: (h, 0, 0))],
        out_specs=[pl.BlockSpec((n_rows, KV_LORA), lambda h: (0, h)),
                   pl.BlockSpec((n_rows, LANES), lambda h: (0, h))],
        out_shape=[jax.ShapeDtypeStruct((n_rows, N_HEADS * KV_LORA), BF16),
                   jax.ShapeDtypeStruct((n_rows, N_HEADS * LANES), BF16)],
        compiler_params=_params(),
        name="qabs",
    )(q, wukt)


def _decode_kernel(pt_ref, qa_ref, qr_ref, kvn_ref, krn_ref, kv_hbm, krt_hbm, o_ref,
                   kv_buf, krt_buf, m_ref, l_ref, acc_ref, kv_sem, kr_sem, *, n_chunks, heads):
    b = pl.program_id(0)
    n_seq = pl.num_programs(0)
    n_slots, cp = kv_buf.shape[0], kv_buf.shape[1]
    ahead = n_slots - 1
    per_stream = cp // DECODE_STREAMS

    def page_copies(g, slot, i):
        page = pt_ref[g // n_chunks, (g % n_chunks) * cp + i]
        return (pltpu.make_async_copy(kv_hbm.at[0, page], kv_buf.at[slot, i], kv_sem.at[slot]),
                pltpu.make_async_copy(krt_hbm.at[0, page], krt_buf.at[slot, i], kr_sem.at[slot]))

    def start_chunk(g, slot):
        for i in range(cp):
            kv_copy, kr_copy = page_copies(g, slot, i)
            kv_copy.start(priority=i % 2)
            kr_copy.start(priority=(i + 1) % 2)

    def wait_chunk(g, slot):
        for i in range(cp):
            for copy in page_copies(g, slot, i):
                copy.wait()

    @pl.when(b == 0)
    def _():
        for c in range(ahead):
            start_chunk(c, c)

    m_ref[...] = jnp.full(m_ref.shape, -jnp.inf, F32)
    l_ref[...] = jnp.zeros(l_ref.shape, F32)
    acc_ref[...] = jnp.zeros(acc_ref.shape, F32)
    qa = qa_ref[...]
    qr = qr_ref[:, 0:ROPE_DIM]

    def consume(slot):
        for t in range(DECODE_STREAMS):
            p0 = t * per_stream
            kvb = kv_buf[slot, p0:p0 + per_stream].reshape(per_stream * PAGE_SIZE, KV_LORA).astype(BF16)
            krt = jnp.concatenate([krt_buf[slot, p0 + i] for i in range(per_stream)], axis=1).astype(BF16)
            s = (_dot_nt(qa, kvb) + _dot(qr, krt)) * SCORE_SCALE_LOG2
            m = m_ref[t]
            m_new = jnp.maximum(m, jnp.max(s, axis=-1, keepdims=True))
            alpha = jnp.exp2(m - m_new)
            p = jnp.exp2(s - m_new)
            l_ref[t] = alpha * l_ref[t] + jnp.sum(p, axis=-1, keepdims=True)
            acc_ref[t] = alpha * acc_ref[t] + _dot(p.astype(BF16), kvb)
            m_ref[t] = m_new

    def ring_turn(cc, carry):
        for slot in range(n_slots):
            g = b * n_chunks + cc * n_slots + slot
            wait_chunk(g, slot)

            @pl.when(g + ahead < n_seq * n_chunks)
            def _():
                start_chunk(g + ahead, (slot + ahead) % n_slots)

            consume(slot)
        return carry

    lax.fori_loop(0, n_chunks // n_slots, ring_turn, 0)

    kvn = kvn_ref[...].astype(BF16)
    s_new = (_dot_nt(qa, kvn) + _dot_nt(qr, krn_ref[...].astype(BF16))) * SCORE_SCALE_LOG2
    tok = lax.broadcasted_iota(jnp.int32, s_new.shape, 0) // heads
    col = lax.broadcasted_iota(jnp.int32, s_new.shape, 1)
    s_new = jnp.where(col <= tok, s_new, -jnp.inf)
    m = jnp.max(s_new, axis=-1, keepdims=True)
    for t in range(DECODE_STREAMS):
        m = jnp.maximum(m, m_ref[t])
    p_new = jnp.exp2(s_new - m)
    l = jnp.sum(p_new, axis=-1, keepdims=True)
    acc = _dot(p_new.astype(BF16), kvn)
    for t in range(DECODE_STREAMS):
        w = jnp.exp2(m_ref[t] - m)
        l = l + w * l_ref[t]
        acc = acc + w * acc_ref[t]
    o_ref[...] = acc / l


def _decode(page_table, qa, qr, kvn, krn, cache_kv, cache_krt):
    bd, n_pages = page_table.shape
    rows = qa.shape[1]
    cp = DECODE_STREAMS * PAGES_PER_STREAM
    n_chunks = n_pages // cp
    ns = DECODE_SLOTS
    assert n_pages % cp == 0 and n_chunks % ns == 0
    per_seq = lambda r, w: pl.BlockSpec((None, r, w), lambda b, pt: (b, 0, 0))
    grid_spec = pltpu.PrefetchScalarGridSpec(
        num_scalar_prefetch=1,
        grid=(bd,),
        in_specs=[per_seq(rows, KV_LORA), per_seq(rows, LANES), per_seq(NEW_ROWS_PAD, KV_LORA),
                  per_seq(NEW_ROWS_PAD, ROPE_DIM), pl.BlockSpec(memory_space=pl.ANY),
                  pl.BlockSpec(memory_space=pl.ANY)],
        out_specs=per_seq(rows, KV_LORA),
        scratch_shapes=[pltpu.VMEM((ns, cp, PAGE_SIZE, KV_LORA), F32), pltpu.VMEM((ns, cp, ROPE_DIM, PAGE_SIZE), F32),
                        pltpu.VMEM((DECODE_STREAMS, rows, 1), F32), pltpu.VMEM((DECODE_STREAMS, rows, 1), F32),
                        pltpu.VMEM((DECODE_STREAMS, rows, KV_LORA), F32),
                        pltpu.SemaphoreType.DMA((ns,)), pltpu.SemaphoreType.DMA((ns,))],
    )
    return pl.pallas_call(
        functools.partial(_decode_kernel, n_chunks=n_chunks, heads=N_HEADS),
        grid_spec=grid_spec,
        out_shape=jax.ShapeDtypeStruct((bd, rows, KV_LORA), F32),
        compiler_params=_params(),
        name="decode",
    )(page_table, qa, qr, kvn, krn, cache_kv, cache_krt)


def _ov_kernel(olat_ref, wuv_ref, o_ref):
    o_ref[...] = _dot(olat_ref[...].astype(BF16), wuv_ref[...])


def _ov(olat2d, wuv_h):
    n_rows = olat2d.shape[0]
    return pl.pallas_call(
        _ov_kernel,
        grid=(N_HEADS,),
        in_specs=[pl.BlockSpec((n_rows, KV_LORA), lambda h: (0, h)),
                  pl.BlockSpec((None, KV_LORA, V_DIM), lambda h: (h, 0, 0))],
        out_specs=pl.BlockSpec((n_rows, V_DIM), lambda h: (0, h)),
        out_shape=jax.ShapeDtypeStruct((n_rows, N_HEADS * V_DIM), F32),
        compiler_params=_params(),
        name="ov",
    )(olat2d, wuv_h)


def _post_kernel(xp_ref, xs_ref, pp_ref, ps_ref, op_ref, os_ref, gmix_ref, wg_ref, wpool_ref, pscale_ref, wo_ref,
                 gffn_ref, wrhi_ref, wrlo_ref, br_ref, x1_ref, hn_ref, ridx_ref, rw_ref, merged_ref, *,
                 prompt_tiles):
    is_prompt = pl.program_id(0) < prompt_tiles
    x = jnp.where(is_prompt, xp_ref[...], xs_ref[...])
    xn = _rms(x, gmix_ref[...]).astype(BF16)
    w = POOL_OUT_GROUP_DIM
    for g in range(len(POOL_WINDOWS)):
        cols = slice(g * w, (g + 1) * w)
        pcols = slice(g * POOL_GROUP_DIM, (g + 1) * POOL_GROUP_DIM)
        pooled = jnp.where(is_prompt, pp_ref[:, pcols], ps_ref[:, pcols]).astype(BF16)
        y_pool = _dot(pooled, wpool_ref[g]) * pscale_ref[:, cols]
        gate_pool = jax.nn.sigmoid(_dot(xn, wg_ref[:, cols]))
        gate_mla = jax.nn.sigmoid(_dot(xn, wg_ref[:, D_MODEL + g * w:D_MODEL + (g + 1) * w]))
        y_mla = jnp.where(is_prompt, op_ref[:, cols], os_ref[:, cols])
        merged_ref[:, cols] = (gate_pool * y_pool + gate_mla * y_mla).astype(BF16)
    x1 = x + _dot(merged_ref[...], wo_ref[...])
    x1_ref[...] = x1
    hn = _rms(x1, gffn_ref[...])
    _store_token_major(hn_ref, hn)

    hi = hn.astype(BF16)
    lo = (hn - hi.astype(F32)).astype(BF16)
    logits = _dot(hi, wrhi_ref[...]) + (_dot(hi, wrlo_ref[...]) + _dot(lo, wrhi_ref[...])) + br_ref[...]

    lane = lax.broadcasted_iota(jnp.int32, logits.shape, 1)
    lane_f = lane.astype(F32)
    none = float(LANES)
    gl = jnp.where(lane < N_GROUPS, logits, -jnp.inf)
    g_max = jnp.max(gl, axis=-1, keepdims=True)
    g_sel = jnp.min(jnp.where(gl == g_max, lane_f, none), axis=-1, keepdims=True)
    g_w = 1.0 / jnp.sum(jnp.exp(gl - g_max), axis=-1, keepdims=True)
    first = N_GROUPS + EXPERTS_PER_GROUP * g_sel
    el = jnp.where((lane_f >= first) & (lane_f < first + EXPERTS_PER_GROUP), logits, -jnp.inf)
    t1 = jnp.max(el, axis=-1, keepdims=True)
    i1 = jnp.min(jnp.where(el == t1, lane_f, none), axis=-1, keepdims=True)
    el2 = jnp.where(lane_f == i1, -jnp.inf, el)
    t2 = jnp.max(el2, axis=-1, keepdims=True)
    i2 = jnp.min(jnp.where(el2 == t2, lane_f, none), axis=-1, keepdims=True)
    e2 = jnp.exp(t2 - t1)
    w1 = g_w / (1.0 + e2)
    w2 = g_w * e2 / (1.0 + e2)
    ridx = jnp.where(lane == 0, i1 - N_GROUPS, jnp.where(lane == 1, i2 - N_GROUPS, 0.0))
    ridx_ref[...] = ridx.astype(jnp.int32)
    rw_ref[...] = jnp.where(lane == 0, w1, jnp.where(lane == 1, w2, 0.0))


def _post(x_p, x_s, pooled_p, pooled_s, o_prompt, o_sample, gmix, wg, wpool, pscale, wo, gffn, wrhi, wrlo, br):
    t = x_p.shape[0] + x_s.shape[0]
    tm = POST_TILE
    n_p = x_p.shape[0] // tm
    row = lambda w: pl.BlockSpec((tm, w), lambda i: (i, 0))
    return pl.pallas_call(
        functools.partial(_post_kernel, prompt_tiles=n_p),
        grid=(t // tm,),
        in_specs=[*_split_rows(tm, D_MODEL, n_p), *_split_rows(tm, POOL_DIM, n_p), *_split_rows(tm, D_MODEL, n_p),
                  _resident(gmix.shape), _resident(wg.shape),
                  _resident(wpool.shape), _resident(pscale.shape), _resident(wo.shape), _resident(gffn.shape),
                  _resident(wrhi.shape), _resident(wrlo.shape), _resident(br.shape)],
        out_specs=[row(D_MODEL), pl.BlockSpec((tm * SUBROWS, LANES), lambda i: (i, 0)), row(LANES), row(LANES)],
        out_shape=[jax.ShapeDtypeStruct((t, D_MODEL), F32), jax.ShapeDtypeStruct((t * SUBROWS, LANES), F32),
                   jax.ShapeDtypeStruct((t, LANES), jnp.int32), jax.ShapeDtypeStruct((t, LANES), F32)],
        scratch_shapes=[pltpu.VMEM((tm, D_MODEL), BF16)],
        compiler_params=_params(),
        name="post",
    )(x_p, x_s, pooled_p, pooled_s, o_prompt, o_sample, gmix, wg, wpool, pscale, wo, gffn, wrhi, wrlo, br)


def _store_token_major(ref, x):
    rows = x.shape[0]
    for s in range(SUBROWS):
        ref[pl.ds(s, rows, stride=SUBROWS), :] = x[:, s * LANES:(s + 1) * LANES]


def _load_token_major(ref, rows):
    return jnp.concatenate([ref[pl.ds(s, rows, stride=SUBROWS), :] for s in range(SUBROWS)], axis=1)


def _row_copy(src_hbm, src_row, dst_ref, dst_row, sem):
    def token_rows(row):
        first = row * SUBROWS
        return pl.ds(first if isinstance(first, int) else pl.multiple_of(first, SUBROWS), SUBROWS)

    return pltpu.make_async_copy(src_hbm.at[token_rows(src_row), :], dst_ref.at[token_rows(dst_row), :], sem)


def _moe_kernel(be_ref, nbu_ref, tok_next_ref, tok_first_ref, hn_hbm, wg_hbm, wu_hbm, wd_hbm, o_ref,
                x_buf, act_ref, gu_buf, d_buf, x_sem, g_sem, u_sem, d_sem):
    b = pl.program_id(0)
    n_used = nbu_ref[0]
    bm = x_buf.shape[1] // SUBROWS
    kg = gu_buf.shape[2]
    n_gu = D_MODEL // kg
    kd = d_buf.shape[1]
    down_stages = [(r0, min(kd, D_EXPERT - r0)) for r0 in range(0, D_EXPERT, kd)]
    n_dn = len(down_stages)
    assert bm % n_gu == 0

    def gate_up_copies(e, j, slot):
        half = kg // 2
        copies = []
        for m, (w_hbm, sem) in enumerate(((wg_hbm, g_sem), (wu_hbm, u_sem))):
            for h in range(2):
                src = w_hbm.at[0, e, pl.ds(j * kg + h * half, half), :]
                copies.append((pltpu.make_async_copy(src, gu_buf.at[slot, m, pl.ds(h * half, half), :], sem.at[slot]),
                               (m + h) % 2))
        return copies

    def down_copies(e, c, slot):
        r0, n = down_stages[c]
        half = n // 2
        return [(pltpu.make_async_copy(wd_hbm.at[0, e, pl.ds(r0 + h * half, half), :],
                                       d_buf.at[slot, pl.ds(h * half, half), :], d_sem.at[slot]), h) for h in range(2)]

    def start_all(copies):
        for copy, priority in copies:
            copy.start(priority=priority)

    def wait_all(copies):
        for copy, _ in copies:
            copy.wait()

    def start_gate_up(e, j, slot):
        start_all(gate_up_copies(e, j, slot))

    def start_rows(tok_ref, slot, r0, r1):
        for r in range(r0, r1):
            _row_copy(hn_hbm, tok_ref[0, 0, r], x_buf.at[slot], r, x_sem.at[slot]).start(priority=r % 2)

    def wait_rows(slot):
        def body(r, carry):
            _row_copy(hn_hbm, 0, x_buf.at[slot], r, x_sem.at[slot]).wait()
            return carry
        lax.fori_loop(0, bm, body, 0, unroll=8)

    ring = gu_buf.shape[0]
    first_slot = lax.rem(b * n_gu, ring)

    def gu_slot(j):
        return lax.rem(first_slot + j, ring)

    @pl.when(b == 0)
    def _():
        start_gate_up(be_ref[0], 0, 0)
        start_gate_up(be_ref[0], 1, 1)

        def body(r, carry):
            _row_copy(hn_hbm, tok_first_ref[0, 0, r], x_buf.at[0], r, x_sem.at[0]).start()
            return carry
        lax.fori_loop(0, bm, body, 0)

    @pl.when(b < n_used)
    def _():
        e = be_ref[b]
        e_next = be_ref[b + 1]
        x_slot = b % 2
        start_all(down_copies(e, 0, 0))
        start_all(down_copies(e, 1, 1))
        wait_rows(x_slot)
        xb = _load_token_major(x_buf.at[x_slot], bm).astype(BF16)
        rows_per_stage = bm // n_gu

        def fetch_gate_up(j):
            slot = gu_slot(j)
            wait_all(gate_up_copies(e, j, slot))
            if j + 2 < n_gu:
                start_gate_up(e, j + 2, gu_slot(j + 2))
            else:
                start_gate_up(e_next, j + 2 - n_gu, gu_slot(j + 2))
            return gu_buf[slot, 0].astype(BF16), gu_buf[slot, 1].astype(BF16)

        def fetch_down(c):
            slot = c % 2
            wait_all(down_copies(e, c, slot))
            w = d_buf[slot, 0:down_stages[c][1]].astype(BF16)
            if c + 2 < n_dn:
                start_all(down_copies(e, c + 2, slot))
            return w

        w = fetch_gate_up(0)
        h_gate = h_up = None
        for j in range(n_gu):
            w_next = fetch_gate_up(j + 1) if j + 1 < n_gu else fetch_down(0)
            start_rows(tok_next_ref, 1 - x_slot, j * rows_per_stage, (j + 1) * rows_per_stage)
            xk = xb[:, j * kg:(j + 1) * kg]
            d_gate, d_up = _dot(xk, w[0]), _dot(xk, w[1])
            h_gate = d_gate if j == 0 else h_gate + d_gate
            h_up = d_up if j == 0 else h_up + d_up
            w = w_next
        act_ref[...] = (jax.nn.silu(h_gate) * h_up).astype(BF16)
        out = None
        for c in range(n_dn):
            w_next = fetch_down(c + 1) if c + 1 < n_dn else None
            r0, n = down_stages[c]
            part = _dot(act_ref[:, r0:r0 + n], w)
            out = part if c == 0 else out + part
            w = w_next
        o_ref[...] = out

    @pl.when(b == n_used)
    def _():
        wait_rows(b % 2)
        for j in range(2):
            wait_all(gate_up_copies(be_ref[b], j, gu_slot(j)))

    @pl.when(b >= n_used)
    def _():
        o_ref[...] = jnp.zeros(o_ref.shape, o_ref.dtype)


def _moe(block_expert, n_used, slot_tok, hn, w_gate, w_up, w_down):
    nb, _, bm = slot_tok.shape
    tok_spec = lambda index_map: pl.BlockSpec((1, 1, bm), index_map, memory_space=pltpu.SMEM)
    grid_spec = pltpu.PrefetchScalarGridSpec(
        num_scalar_prefetch=2,
        grid=(nb,),
        in_specs=[tok_spec(lambda b, be, nbu: (jnp.minimum(b + 1, nb - 1), 0, 0)),
                  tok_spec(lambda b, be, nbu: (0, 0, 0)),
                  pl.BlockSpec(memory_space=pl.ANY), pl.BlockSpec(memory_space=pl.ANY),
                  pl.BlockSpec(memory_space=pl.ANY), pl.BlockSpec(memory_space=pl.ANY)],
        out_specs=pl.BlockSpec((bm, D_MODEL), lambda b, be, nbu: (b, 0)),
        scratch_shapes=[pltpu.VMEM((2, bm * SUBROWS, LANES), F32), pltpu.VMEM((bm, D_EXPERT), BF16),
                        pltpu.VMEM((GATE_UP_RING, 2, GATE_UP_ROWS, D_EXPERT), F32),
                        pltpu.VMEM((2, DOWN_ROWS, D_MODEL), F32),
                        pltpu.SemaphoreType.DMA((2,)), pltpu.SemaphoreType.DMA((GATE_UP_RING,)),
                        pltpu.SemaphoreType.DMA((GATE_UP_RING,)), pltpu.SemaphoreType.DMA((2,))],
    )
    return pl.pallas_call(
        _moe_kernel,
        grid_spec=grid_spec,
        out_shape=jax.ShapeDtypeStruct((nb * bm, D_MODEL), F32),
        compiler_params=_params(),
        name="moe",
    )(block_expert, n_used, slot_tok, slot_tok, hn, w_gate, w_up, w_down)


def _combine_kernel(pos_first_ref, pos_next_ref, x1_ref, rw_ref, gfin_ref, ys_hbm, yp_ref, ys_ref, buf_ref, sem, *,
                    prompt_tiles):
    i = pl.program_id(0)
    last = pl.num_programs(0) - 1
    n = x1_ref.shape[0]
    slot = i % 2

    def copy(pos_ref, dst_slot, r, k):
        src = ys_hbm.at[pl.ds(pos_ref[0, 0, TOP_K * r + k], 1), :]
        return pltpu.make_async_copy(src, buf_ref.at[dst_slot, k, pl.ds(r, 1), :], sem.at[dst_slot])

    def wait_slot(dst_slot):
        def body(r, carry):
            for k in range(TOP_K):
                copy(pos_next_ref, dst_slot, r, k).wait()
            return carry
        lax.fori_loop(0, n, body, 0, unroll=8)

    @pl.when(i == 0)
    def _():
        def body(r, carry):
            for k in range(TOP_K):
                copy(pos_first_ref, 0, r, k).start(priority=k)
            return carry
        lax.fori_loop(0, n, body, 0)

    wait_slot(slot)
    for r in range(n):
        for k in range(TOP_K):
            copy(pos_next_ref, 1 - slot, r, k).start(priority=k)
    rw = rw_ref[...]
    moe = rw[:, 0:1] * buf_ref[slot, 0] + rw[:, 1:2] * buf_ref[slot, 1]
    y = _rms(x1_ref[...] + moe, gfin_ref[...])

    @pl.when(i == last)
    def _():
        wait_slot(1 - slot)

    is_prompt = i < prompt_tiles

    @pl.when(is_prompt)
    def _():
        yp_ref[...] = y

    @pl.when(jnp.logical_not(is_prompt))
    def _():
        ys_ref[...] = y


def _combine(pos, x1, rw, gfin, ys, n_prompt_rows):
    t = x1.shape[0]
    tm = COMBINE_TILE
    n_p = n_prompt_rows // tm
    n_tiles = t // tm
    pos_spec = lambda index_map: pl.BlockSpec((1, 1, TOP_K * tm), index_map, memory_space=pltpu.SMEM)
    return pl.pallas_call(
        functools.partial(_combine_kernel, prompt_tiles=n_p),
        grid=(n_tiles,),
        in_specs=[pos_spec(lambda i: (0, 0, 0)),
                  pos_spec(lambda i: (jnp.minimum(i + 1, n_tiles - 1), 0, 0)),
                  pl.BlockSpec((tm, D_MODEL), lambda i: (i, 0)),
                  pl.BlockSpec((tm, LANES), lambda i: (i, 0)),
                  pl.BlockSpec(gfin.shape, lambda i: (0, 0)),
                  pl.BlockSpec(memory_space=pl.ANY)],
        out_specs=list(_split_rows(tm, D_MODEL, n_p)),
        out_shape=[jax.ShapeDtypeStruct((n_prompt_rows, D_MODEL), F32),
                   jax.ShapeDtypeStruct((t - n_prompt_rows, D_MODEL), F32)],
        scratch_shapes=[pltpu.VMEM((2, TOP_K, tm, D_MODEL), F32), pltpu.SemaphoreType.DMA((2,))],
        compiler_params=_params(),
        name="combine",
    )(pos, pos, x1, rw, gfin, ys)


def _dispatch(experts, bm):
    t = experts.shape[0]
    a = t * TOP_K
    nb = -(-(a + N_EXPERTS * (bm - 1)) // bm) + 1
    e_flat = experts.reshape(a)
    onehot = (e_flat[:, None] == jnp.arange(N_EXPERTS, dtype=jnp.int32)[None, :]).astype(jnp.int32)
    csum = jnp.cumsum(onehot, axis=0)
    rank = jnp.sum(csum * onehot, axis=1) - 1
    counts = csum[-1]
    padded = (counts + bm - 1) // bm * bm
    pend = jnp.cumsum(padded)
    pstart = pend - padded
    dest = pstart[e_flat] + rank
    n_used = (pend[-1] // bm).astype(jnp.int32)
    blk = jnp.arange(nb, dtype=jnp.int32)
    be = jnp.minimum(jnp.searchsorted(pend, blk * bm, side='right'), N_EXPERTS - 1).astype(jnp.int32)
    be = jnp.where(blk < n_used, be, be[jnp.maximum(n_used - 1, 0)])
    slot_tok = jnp.zeros((nb * bm,), jnp.int32).at[dest].set(jnp.arange(a, dtype=jnp.int32) // TOP_K)
    return dest.astype(jnp.int32), slot_tok.reshape(nb, 1, bm), be, n_used.reshape(1)


def _rope_table(pos):
    inv = 1.0 / (ROPE_THETA ** (jnp.arange(0, ROPE_DIM, 2, dtype=F32) / ROPE_DIM))
    ang = pos.astype(F32)[:, None] * inv[None, :]
    cos, sin = jnp.cos(ang), jnp.sin(ang)
    z = jnp.zeros((pos.shape[0], LANES - ROPE_DIM), F32)
    return jnp.concatenate([cos, cos, z, sin, sin, z], axis=1)


def _rotate_half_cols(w):
    return jnp.concatenate([-w[..., HALF_ROPE:], w[..., :HALF_ROPE]], axis=-1)


def kernel(x_prompt, x_sample, state_pool, cache_kv_latent, cache_k_rope, page_table, g_mix_norm, w_in, g_q_norm,
           g_kv_norm, w_uq, w_uk, w_uv, w_pool, pool_scale, w_o, g_ffn_norm, w_group_router, b_group_router,
           w_router, b_router, w_gate_e, w_up_e, w_down_e, g_final_norm):
    n_batch, seq, d = x_prompt.shape
    bd, sd, _ = x_sample.shape
    assert w_in.shape[0] == 1 and d == D_MODEL, "single-layer step only"
    n_pages = page_table.shape[1]
    past_len = n_pages * PAGE_SIZE
    tp, ts = n_batch * seq, bd * sd
    t = tp + ts
    assert tp % ts == 0 and seq % ROW_TILE == 0 and t % ROW_TILE == 0 and ts % ROW_TILE == 0

    w_in0 = w_in[0]
    o_kr = POOL_DIM + Q_LORA + KV_LORA
    wa = w_in0[:, :o_kr].astype(BF16)
    w_kr = w_in0[:, o_kr:o_kr + ROPE_DIM]
    zk = jnp.zeros((d, LANES - ROPE_DIM), F32)
    wkr = jnp.concatenate([w_kr, zk, _rotate_half_cols(w_kr), zk], axis=1).astype(BF16)
    wg = w_in0[:, o_kr + ROPE_DIM:].astype(BF16)
    nope, rope = w_uq[0][:, :, :NOPE_DIM], w_uq[0][:, :, NOPE_DIM:]
    zq = jnp.zeros((Q_LORA, N_HEADS, LANES - ROPE_DIM), F32)
    wq = jnp.concatenate([nope, rope, zq], axis=-1).reshape(Q_LORA, N_HEADS * QK_DIM).astype(BF16)
    wqp = jnp.concatenate([_rotate_half_cols(rope), zq], axis=-1).reshape(Q_LORA, N_HEADS * LANES).astype(BF16)
    wuk = w_uk[0].reshape(KV_LORA, N_HEADS * NOPE_DIM).astype(BF16)
    wuv = w_uv[0].reshape(KV_LORA, N_HEADS * V_DIM).astype(BF16)
    wukt = w_uk[0].transpose(1, 2, 0).astype(BF16)
    wuv_h = w_uv[0].transpose(1, 0, 2).astype(BF16)
    wr = jnp.concatenate([w_group_router[0], w_router[0],
                          jnp.zeros((d, LANES - N_GROUPS - N_EXPERTS), F32)], axis=1)
    wrhi = wr.astype(BF16)
    wrlo = (wr - wrhi.astype(F32)).astype(BF16)
    br = jnp.concatenate([b_group_router[0], b_router[0],
                          jnp.zeros((LANES - N_GROUPS - N_EXPERTS,), F32)]).reshape(1, LANES)
    row = lambda g: g.reshape(1, -1)

    pos = jnp.concatenate([jnp.tile(jnp.arange(seq, dtype=jnp.int32), n_batch),
                           past_len + jnp.tile(jnp.arange(sd, dtype=jnp.int32), bd)])
    cs = _rope_table(pos)

    x_p, x_s = x_prompt.reshape(tp, d), x_sample.reshape(ts, d)
    u, kv, kr, q, k, v = _inproj(x_p, x_s, row(g_mix_norm[0]), wa, wkr, row(g_q_norm[0]), row(g_kv_norm[0]),
                                 wq, wqp, wuk, wuv, cs)

    u_s = u[tp:].reshape(bd, sd, POOL_DIM)
    slab = HALO + 8
    u_full_s = jnp.concatenate([state_pool[0].astype(F32), u_s,
                                jnp.zeros((bd, slab - POOL_STATE_LEN - sd, POOL_DIM), F32)], axis=1)
    pooled_p = _pool(u, tp, seq)
    pooled_s = _pool(u_full_s.reshape(bd * slab, POOL_DIM), bd * slab, None)
    pooled_s = pooled_s.reshape(bd, slab, POOL_DIM)[:, POOL_STATE_LEN:POOL_STATE_LEN + sd].reshape(ts, POOL_DIM)

    o_prompt = _flash(q, k, v, n_batch, seq)
    qa, qr = _qabs(q, wukt, tp, ts)
    qa = qa.reshape(bd, sd * N_HEADS, KV_LORA)
    qr = qr.reshape(bd, sd * N_HEADS, LANES)
    pad_new = lambda z: jnp.pad(z.reshape(bd, sd, -1), ((0, 0), (0, NEW_ROWS_PAD - sd), (0, 0)))
    o_lat = _decode(page_table, qa, qr, pad_new(kv[tp:]), pad_new(kr[tp:]), cache_kv_latent,
                    jnp.swapaxes(cache_k_rope, 2, 3))
    o_sample = _ov(o_lat.reshape(ts, N_HEADS * KV_LORA), wuv_h)

    x1, hn, ridx, rw = _post(x_p, x_s, pooled_p, pooled_s, o_prompt, o_sample, row(g_mix_norm[0]), wg,
                             w_pool[0].astype(BF16), row(pool_scale[0]), w_o[0].astype(BF16), row(g_ffn_norm[0]),
                             wrhi, wrlo, br)

    dest, slot_tok, block_expert, n_used = _dispatch(ridx[:, :TOP_K], MOE_BLOCK)
    ys = _moe(block_expert, n_used, slot_tok, hn, w_gate_e, w_up_e, w_down_e)
    y_p, y_s = _combine(dest.reshape(t // COMBINE_TILE, 1, TOP_K * COMBINE_TILE), x1, rw, row(g_final_norm), ys, tp)

    pool_p = jnp.stack([u[(i + 1) * seq - POOL_STATE_LEN:(i + 1) * seq] for i in range(n_batch)])
    pool_s = jnp.concatenate([state_pool[0].astype(F32), u_s], axis=1)[:, -POOL_STATE_LEN:]
    return (y_p.reshape(n_batch, seq, d), y_s.reshape(bd, sd, d),
            kv[:tp].reshape(1, n_batch, seq, KV_LORA), kr[:tp].reshape(1, n_batch, seq, ROPE_DIM),
            pool_p[None],
            kv[tp:].reshape(1, bd, sd, KV_LORA), kr[tp:].reshape(1, bd, sd, ROPE_DIM), pool_s[None])
```
